```python
import math
import jax, jax.numpy as jnp
from jax import lax
import numpy as np


D_MODEL = 1024
BATCH = 32
SEQ = 2048
DEPTH = 2

PLE_DIM = 256
EPS = 1e-6
ROPE_THETA = 10000.0

CONV_WIDTH = D_MODEL // 2
CONV_K = 3
ATT_HEADS = 8
ATT_HEAD_DIM = 64
ATT_WIDTH = ATT_HEADS * ATT_HEAD_DIM
IDX_HEADS = 8
IDX_DIM = 64
IDX_ROPE_DIM = 32
TOPK_MAX = 256
Q_BLOCK = 128
DN_HEADS = 8
DN_DK = 128
DN_DV = 128
DN_CONV_K = 4
DN_CHUNK = 64
D_FF = -(-8 * D_MODEL // (3 * 256)) * 256

EVEN_SPLITS = (CONV_WIDTH, CONV_WIDTH, CONV_WIDTH, ATT_WIDTH, ATT_WIDTH, ATT_WIDTH, IDX_HEADS * IDX_DIM, IDX_DIM, IDX_HEADS)
EVEN_IN = sum(EVEN_SPLITS)
EVEN_MIX = CONV_WIDTH + ATT_WIDTH
DN_QKV = 2 * DN_HEADS * DN_DK + DN_HEADS * DN_DV
ODD_SPLITS = (DN_QKV, DN_HEADS * DN_DV, DN_HEADS, DN_HEADS)
ODD_IN = sum(ODD_SPLITS)
ODD_MIX = DN_HEADS * DN_DV
N_EVEN = (DEPTH + 1) // 2
N_ODD = DEPTH // 2

kernel_name = 'hybrid_conv_dsa_deltanet_block'


def split_cols(z, sizes):
    return jnp.split(z, [int(c) for c in np.cumsum(sizes)[:-1]], axis=-1)


def rmsnorm(x, g):
    xf = x.astype(jnp.float32)
    y = xf * lax.rsqrt(jnp.mean(xf * xf, axis=-1, keepdims=True) + EPS)
    return (y * g.astype(jnp.float32)).astype(x.dtype)


def layernorm(x, g, b):
    xf = x.astype(jnp.float32)
    mu = jnp.mean(xf, axis=-1, keepdims=True)
    var = jnp.mean(jnp.square(xf - mu), axis=-1, keepdims=True)
    y = (xf - mu) * lax.rsqrt(var + EPS)
    return (y * g.astype(jnp.float32) + b.astype(jnp.float32)).astype(x.dtype)


def l2norm(x):
    return x * lax.rsqrt(jnp.sum(x * x, axis=-1, keepdims=True) + EPS)


def rope_tables(positions, dim):
    inv_freq = 1.0 / (ROPE_THETA ** (jnp.arange(0, dim, 2, dtype=jnp.float32) / dim))
    ang = positions.astype(jnp.float32)[..., None] * inv_freq
    return jnp.cos(ang), jnp.sin(ang)


def apply_rope(x, cos, sin):
    x1, x2 = jnp.split(x.astype(jnp.float32), 2, axis=-1)
    c = cos[:, :, None, :]
    s = sin[:, :, None, :]
    return jnp.concatenate([x1 * c - x2 * s, x1 * s + x2 * c], axis=-1).astype(x.dtype)


def partial_rope(x, cos, sin):
    return jnp.concatenate([apply_rope(x[..., :IDX_ROPE_DIM], cos, sin), x[..., IDX_ROPE_DIM:]], axis=-1)


def causal_dwconv(u, w):
    k, s = w.shape[0], u.shape[1]
    up = jnp.pad(u, ((0, 0), (k - 1, 0), (0, 0)))
    y = up[:, 0:s] * w[0]
    for j in range(1, k):
        y = y + up[:, j:j + s] * w[j]
    return y


def dsa_attention(q, k, v, qi, ki, wi):
    b_, s_, _, dh = q.shape
    topk = min(TOPK_MAX, s_ // 4)
    nb = s_ // Q_BLOCK
    key_pos = jnp.arange(s_)
    wi = wi * (IDX_HEADS ** -0.5 * IDX_DIM ** -0.5)

    def to_blocks(t):
        return jnp.swapaxes(t.reshape(b_, nb, Q_BLOCK, *t.shape[2:]), 0, 1)

    def block(args):
        qb, qib, wib, start = args
        qpos = start + jnp.arange(Q_BLOCK)
        causal = key_pos[None, :] <= qpos[:, None]
        rel = jax.nn.relu(jnp.einsum('bqhd,bkd->bqhk', qib, ki).astype(jnp.float32))
        score = jnp.einsum('bqhk,bqh->bqk', rel, wib.astype(jnp.float32))
        score = jnp.where(causal[None], score, -jnp.inf)
        _, idx = lax.top_k(score, topk)
        valid = idx <= qpos[None, :, None]
        kg = jax.vmap(lambda kk, ii: kk[ii])(k, idx)
        vg = jax.vmap(lambda vv, ii: vv[ii])(v, idx)
        logits = jnp.einsum('bqhd,bqnhd->bhqn', qb, kg).astype(jnp.float32) * (dh ** -0.5)
        logits = jnp.where(valid[:, None], logits, -jnp.inf)
        prob = jax.nn.softmax(logits, axis=-1)
        return jnp.einsum('bhqn,bqnhd->bqhd', prob.astype(vg.dtype), vg)

    starts = jnp.arange(nb, dtype=jnp.int32) * Q_BLOCK
    out = lax.map(block, (to_blocks(q), to_blocks(qi), to_blocks(wi), starts))
    return jnp.swapaxes(out, 0, 1).reshape(b_, s_, q.shape[2], dh)


def conv_dsa_mixer(h, cos_a, sin_a, cos_i, sin_i, w_in, conv_w, q_norm_g, k_norm_g, ik_ln_g, ik_ln_b, w_out):
    b_, s_, _ = h.shape
    zb, zc, zh, q, k, v, qi, ki, wi = split_cols(h @ w_in, EVEN_SPLITS)
    y_a = zb * causal_dwconv(zc * zh, conv_w)
    heads = lambda t: t.reshape(b_, s_, ATT_HEADS, ATT_HEAD_DIM)
    q = apply_rope(rmsnorm(heads(q), q_norm_g), cos_a, sin_a)
    k = apply_rope(rmsnorm(heads(k), k_norm_g), cos_a, sin_a)
    v = heads(v)
    qi = partial_rope(qi.reshape(b_, s_, IDX_HEADS, IDX_DIM), cos_i, sin_i)
    ki = partial_rope(layernorm(ki, ik_ln_g, ik_ln_b)[:, :, None, :], cos_i, sin_i)[:, :, 0, :]
    y_b = dsa_attention(q, k, v, qi, ki, wi).reshape(b_, s_, ATT_WIDTH)
    return jnp.concatenate([y_a, y_b], axis=-1) @ w_out


def gated_delta_rule(q, k, v, g, beta):
    b_, s_, h_, dk = q.shape
    dv = v.shape[-1]
    c = DN_CHUNK
    n = s_ // c
    q = l2norm(q) * (dk ** -0.5)
    k = l2norm(k)

    def chunks(t):
        t = jnp.moveaxis(t, 2, 1)
        return t.reshape(b_, h_, n, c, *t.shape[3:])

    q, k, v, g, beta = chunks(q), chunks(k), chunks(v), chunks(g), chunks(beta)
    g = jnp.cumsum(g, axis=-1)
    tril = jnp.tril(jnp.ones((c, c), bool))
    strict = jnp.tril(jnp.ones((c, c), bool), -1)
    decay = jnp.exp(jnp.where(tril, g[..., :, None] - g[..., None, :], -jnp.inf))
    kb = k * beta[..., None]
    vb = v * beta[..., None]
    m = jnp.where(strict, jnp.einsum('bhncd,bhnsd->bhncs', kb, k) * decay, 0.0)
    eye = jnp.eye(c, dtype=q.dtype)
    t_inv = lax.linalg.triangular_solve(eye + m, jnp.broadcast_to(eye, m.shape), left_side=True, lower=True, unit_diagonal=True)
    u = t_inv @ vb
    w = t_inv @ (kb * jnp.exp(g)[..., None])
    qk = jnp.where(tril, jnp.einsum('bhncd,bhnsd->bhncs', q, k) * decay, 0.0)

    def step(state, xs):
        q_c, k_c, u_c, w_c, g_c, qk_c = xs
        v_new = u_c - w_c @ state
        o = (q_c * jnp.exp(g_c)[..., None]) @ state + qk_c @ v_new
        g_last = g_c[..., -1]
        k_dec = k_c * jnp.exp(g_last[..., None] - g_c)[..., None]
        state = state * jnp.exp(g_last)[..., None, None] + jnp.einsum('bhcd,bhce->bhde', k_dec, v_new)
        return state, o

    xs = tuple(jnp.moveaxis(t, 2, 0) for t in (q, k, u, w, g, qk))
    state0 = jnp.zeros((b_, h_, dk, dv), q.dtype)
    _, o = lax.scan(step, state0, xs)
    o = jnp.moveaxis(o, 0, 2).reshape(b_, h_, s_, dv)
    return jnp.moveaxis(o, 1, 2)


def gated_deltanet_mixer(h, w_in, conv_w, a_log, dt_bias, o_norm_g, w_out):
    b_, s_, _ = h.shape
    qkv, z, a, bt = split_cols(h @ w_in, ODD_SPLITS)
    qkv = jax.nn.silu(causal_dwconv(qkv, conv_w))
    q, k, v = split_cols(qkv, (DN_HEADS * DN_DK, DN_HEADS * DN_DK, DN_HEADS * DN_DV))
    f32 = jnp.float32
    q = q.reshape(b_, s_, DN_HEADS, DN_DK).astype(f32)
    k = k.reshape(b_, s_, DN_HEADS, DN_DK).astype(f32)
    v = v.reshape(b_, s_, DN_HEADS, DN_DV).astype(f32)
    g = -jnp.exp(a_log.astype(f32)) * jax.nn.softplus(a.astype(f32) + dt_bias.astype(f32))
    beta = jax.nn.sigmoid(bt.astype(f32))
    o = gated_delta_rule(q, k, v, g, beta)
    o = rmsnorm(o, o_norm_g) * jax.nn.silu(z.reshape(b_, s_, DN_HEADS, DN_DV).astype(f32))
    return o.reshape(b_, s_, ODD_MIX).astype(h.dtype) @ w_out


def swiglu(h, w_gate, w_up, w_down):
    return (jax.nn.silu(h @ w_gate) * (h @ w_up)) @ w_down


def setup_inputs(seed: int = 0) -> dict:
    key = jax.random.key(seed)
    k = jax.random.split(key, 25)
    f32 = jnp.float32
    nrm = lambda kk, shape, scale: jax.random.normal(kk, shape, f32) * scale
    gain = lambda kk, shape: 1.0 + 0.02 * jax.random.normal(kk, shape, f32)
    x = nrm(k[0], (BATCH, SEQ, D_MODEL), 1.0)
    p = nrm(k[1], (DEPTH, BATCH, SEQ, PLE_DIM), 1.0)
    offset = jax.random.randint(k[2], (BATCH, 1), 0, 4096, dtype=jnp.int32)
    positions = offset + jnp.arange(SEQ, dtype=jnp.int32)[None, :]
    norm_mix_g = gain(k[3], (DEPTH, D_MODEL))
    norm_ffn_g = gain(k[4], (DEPTH, D_MODEL))
    ev_w_in = nrm(k[5], (N_EVEN, D_MODEL, EVEN_IN), D_MODEL ** -0.5)
    ev_conv_w = nrm(k[6], (N_EVEN, CONV_K, CONV_WIDTH), CONV_K ** -0.5)
    ev_q_norm_g = gain(k[7], (N_EVEN, ATT_HEAD_DIM))
    ev_k_norm_g = gain(k[8], (N_EVEN, ATT_HEAD_DIM))
    ev_ik_ln_g = gain(k[9], (N_EVEN, IDX_DIM))
    ev_ik_ln_b = nrm(k[10], (N_EVEN, IDX_DIM), 0.02)
    ev_w_out = nrm(k[11], (N_EVEN, EVEN_MIX, D_MODEL), 0.5 * EVEN_MIX ** -0.5)
    od_w_in = nrm(k[12], (N_ODD, D_MODEL, ODD_IN), D_MODEL ** -0.5)
    od_conv_w = nrm(k[13], (N_ODD, DN_CONV_K, DN_QKV), 0.5)
    dt = jnp.exp(jax.random.uniform(k[14], (N_ODD, DN_HEADS), f32, math.log(1e-3), math.log(1e-1)))
    od_dt_bias = dt + jnp.log(-jnp.expm1(-dt))
    od_a_log = jnp.log(jax.random.uniform(k[15], (N_ODD, DN_HEADS), f32, 1.0, 16.0))
    od_o_norm_g = gain(k[16], (N_ODD, DN_DV))
    od_w_out = nrm(k[17], (N_ODD, ODD_MIX, D_MODEL), 0.5 * ODD_MIX ** -0.5)
    ffn_w_gate = nrm(k[18], (DEPTH, D_MODEL, D_FF), D_MODEL ** -0.5)
    ffn_w_up = nrm(k[19], (DEPTH, D_MODEL, D_FF), D_MODEL ** -0.5)
    ffn_w_down = nrm(k[20], (DEPTH, D_FF, D_MODEL), 0.5 * D_FF ** -0.5)
    ple_w_proj = nrm(k[21], (DEPTH, PLE_DIM, D_MODEL), PLE_DIM ** -0.5)
    ple_post_norm_g = gain(k[22], (DEPTH, D_MODEL))
    ple_norm_g = gain(k[23], (DEPTH, D_MODEL))
    ple_w_gate = nrm(k[24], (DEPTH, D_MODEL, D_MODEL), D_MODEL ** -0.5)
    return {'x': x, 'p': p, 'positions': positions, 'norm_mix_g': norm_mix_g, 'norm_ffn_g': norm_ffn_g,
            'ev_w_in': ev_w_in, 'ev_conv_w': ev_conv_w, 'ev_q_norm_g': ev_q_norm_g, 'ev_k_norm_g': ev_k_norm_g,
            'ev_ik_ln_g': ev_ik_ln_g, 'ev_ik_ln_b': ev_ik_ln_b, 'ev_w_out': ev_w_out,
            'od_w_in': od_w_in, 'od_conv_w': od_conv_w, 'od_a_log': od_a_log, 'od_dt_bias': od_dt_bias,
            'od_o_norm_g': od_o_norm_g, 'od_w_out': od_w_out,
            'ffn_w_gate': ffn_w_gate, 'ffn_w_up': ffn_w_up, 'ffn_w_down': ffn_w_down,
            'ple_w_proj': ple_w_proj, 'ple_post_norm_g': ple_post_norm_g, 'ple_norm_g': ple_norm_g, 'ple_w_gate': ple_w_gate}


def reference(x, p, positions, norm_mix_g, norm_ffn_g, ev_w_in, ev_conv_w, ev_q_norm_g, ev_k_norm_g, ev_ik_ln_g, ev_ik_ln_b, ev_w_out, od_w_in, od_conv_w, od_a_log, od_dt_bias, od_o_norm_g, od_w_out, ffn_w_gate, ffn_w_up, ffn_w_down, ple_w_proj, ple_post_norm_g, ple_norm_g, ple_w_gate):
    cos_a, sin_a = rope_tables(positions, ATT_HEAD_DIM)
    cos_i, sin_i = rope_tables(positions, IDX_ROPE_DIM)
    h = x
    for i in range(DEPTH):
        j = i // 2
        hn = rmsnorm(h, norm_mix_g[i])
        if i % 2 == 0:
            h = h + conv_dsa_mixer(hn, cos_a, sin_a, cos_i, sin_i, ev_w_in[j], ev_conv_w[j], ev_q_norm_g[j],
                                   ev_k_norm_g[j], ev_ik_ln_g[j], ev_ik_ln_b[j], ev_w_out[j])
        else:
            h = h + gated_deltanet_mixer(hn, od_w_in[j], od_conv_w[j], od_a_log[j], od_dt_bias[j],
                                         od_o_norm_g[j], od_w_out[j])
        h = h + swiglu(rmsnorm(h, norm_ffn_g[i]), ffn_w_gate[i], ffn_w_up[i], ffn_w_down[i])
        e = rmsnorm(p[i] @ ple_w_proj[i], ple_post_norm_g[i])
        h = h + e * jax.nn.sigmoid(rmsnorm(h, ple_norm_g[i]) @ ple_w_gate[i])
    return h
```

```python
import functools
import math

import jax
import jax.numpy as jnp
from jax import lax
from jax.experimental import pallas as pl
from jax.experimental.pallas import tpu as pltpu

F32 = jnp.float32
BF16 = jnp.bfloat16
I32 = jnp.int32
I16 = jnp.int16

EPS = 1e-6
ROPE_THETA = 10000.0
LANES = 128
HEAD = 64
N_HEADS = 8
CONV_W = 512
ATT_W = 512
TOPK_MAX = 256
DSA_TQ = 256
DSA_KEY_STEP = 512
DN_HEADS = 8
DN_D = 128
DN_CHUNK = 64
DELTA_TILE = 256
VMEM_LIMIT = 56 * 1024 * 1024

KEY_NEG_INF = -2139095041
INT_MIN = -2147483648
NEG_BIG = -1e30

NT_DIMS = (((1,), (1,)), ((), ()))
TN_DIMS = (((0,), (0,)), ((), ()))


def _dot(a, b):
    return jnp.dot(a, b, preferred_element_type=F32)


def _dot_nt(a, b):
    return lax.dot_general(a, b, NT_DIMS, preferred_element_type=F32)


def _dot_tn(a, b):
    return lax.dot_general(a, b, TN_DIMS, preferred_element_type=F32)


def _rms(x, g):
    ms = jnp.mean(x * x, axis=-1, keepdims=True)
    return x * lax.rsqrt(ms + EPS) * g


def _split3(x):
    x1 = x.astype(BF16)
    r = x - x1.astype(F32)
    x2 = r.astype(BF16)
    x3 = (r - x2.astype(F32)).astype(BF16)
    return x1, x2, x3


def _const_spec(shape):
    nd = len(shape)
    return pl.BlockSpec(shape, lambda *_: (0,) * nd, pipeline_mode=pl.Buffered(1))


def _params(sem):
    return pltpu.CompilerParams(dimension_semantics=sem, vmem_limit_bytes=VMEM_LIMIT)


def _lane_iota():
    return lax.broadcasted_iota(I32, (1, LANES), 1)


def _swap_halves(x, half):
    lane = _lane_iota()
    first = (lane % (2 * half)) < half
    return jnp.where(first, pltpu.roll(x, LANES - half, axis=1), pltpu.roll(x, half, axis=1))


def _even_in_kernel(x_ref, g_ref, w_ref, wx_ref, cw_ref, gq_ref, gk_ref, lng_ref, lnb_ref,
                    ca_ref, sa_ref, ci_ref, si_ref,
                    ya_ref, q_ref, k_ref, v_ref, qi_ref, kx_ref, wi_ref, ubuf_ref, *, tm, wi_scale):
    s = pl.program_id(1)
    hn32 = _rms(x_ref[0], g_ref[...])
    h1, h2, h3 = _split3(hn32)
    hn = h1
    lane = _lane_iota()
    lo = lane < HEAD

    u = _dot(hn, w_ref[:, 512:1024]) * _dot(hn, w_ref[:, 1024:1536])

    @pl.when(s == 0)
    def _():
        ubuf_ref[0:8, :] = jnp.zeros((8, CONV_W), F32)

    @pl.when(s > 0)
    def _():
        ubuf_ref[0:8, :] = ubuf_ref[tm:tm + 8, :]

    ubuf_ref[8:tm + 8, :] = u
    conv = (cw_ref[0:1, :] * ubuf_ref[6:tm + 6, :] + cw_ref[1:2, :] * ubuf_ref[7:tm + 7, :]
            + cw_ref[2:3, :] * u)
    ya_ref[0] = (_dot(hn, w_ref[:, 0:512]) * conv).astype(BF16)

    ca = ca_ref[0]
    sa = sa_ref[0]
    ci = ci_ref[0]
    si = si_ref[0]

    def head_norm_rope(z, gain, scale):
        outs = []
        for j in range(ATT_W // LANES):
            xb = z[:, j * LANES:(j + 1) * LANES]
            sq = xb * xb
            s_lo = jnp.sum(jnp.where(lo, sq, 0.0), axis=-1, keepdims=True)
            s_hi = jnp.sum(jnp.where(lo, 0.0, sq), axis=-1, keepdims=True)
            r = jnp.where(lo, lax.rsqrt(s_lo * (1.0 / HEAD) + EPS), lax.rsqrt(s_hi * (1.0 / HEAD) + EPS))
            xn = xb * r * gain[:, j * LANES:(j + 1) * LANES]
            outs.append((xn * ca + _swap_halves(xn, HEAD // 2) * sa) * scale)
        return jnp.concatenate(outs, axis=-1)

    q_ref[0] = head_norm_rope(_dot(hn, w_ref[:, 1536:2048]), gq_ref[...], HEAD ** -0.5).astype(BF16)
    k_ref[0] = head_norm_rope(_dot(hn, w_ref[:, 2048:2560]), gk_ref[...], 1.0).astype(BF16)
    v_ref[0] = _dot(hn, w_ref[:, 2560:3072]).astype(BF16)

    zidx = _dot(jnp.concatenate([h1, h1, h2, h1, h3, h2], axis=-1), wx_ref[...])
    zqi = zidx[:, 0:ATT_W]
    outs = []
    for j in range(ATT_W // LANES):
        xb = zqi[:, j * LANES:(j + 1) * LANES]
        outs.append(xb * ci + _swap_halves(xb, HEAD // 4) * si)
    qi_ref[0] = jnp.concatenate(outs, axis=-1)

    zkw = zidx[:, ATT_W:ATT_W + LANES]
    mu = jnp.sum(jnp.where(lo, zkw, 0.0), axis=-1, keepdims=True) * (1.0 / HEAD)
    d = zkw - mu
    var = jnp.sum(jnp.where(lo, d * d, 0.0), axis=-1, keepdims=True) * (1.0 / HEAD)
    kn = d * lax.rsqrt(var + EPS) * lng_ref[...] + lnb_ref[...]
    kr = jnp.where(lo, kn * ci + _swap_halves(kn, HEAD // 4) * si, 0.0)
    k1, k2, k3 = (t.astype(F32) for t in _split3(kr))
    k12 = k1 + pltpu.roll(k2, HEAD, axis=1)
    kx_ref[0] = jnp.concatenate([k12, k12, k3 + pltpu.roll(k1, HEAD, axis=1)], axis=-1).astype(BF16)
    wi_ref[0] = jnp.where(lane < N_HEADS, pltpu.roll(zkw, HEAD, axis=1) * wi_scale, 0.0)


def _even_in(x, g, w, wx, cw, gq, gk, lng, lnb, ca, sa, ci, si, *, tm):
    b, s, d = x.shape
    row = lambda width: pl.BlockSpec((1, tm, width), lambda i, j: (i, j, 0))
    out_bf = lambda width: jax.ShapeDtypeStruct((b, s, width), BF16)
    return pl.pallas_call(
        functools.partial(_even_in_kernel, tm=tm, wi_scale=N_HEADS ** -0.5 * HEAD ** -0.5),
        grid=(b, s // tm),
        in_specs=[row(d), _const_spec(g.shape), _const_spec(w.shape), _const_spec(wx.shape), _const_spec(cw.shape),
                  _const_spec(gq.shape), _const_spec(gk.shape), _const_spec(lng.shape), _const_spec(lnb.shape),
                  row(LANES), row(LANES), row(LANES), row(LANES)],
        out_specs=[row(CONV_W), row(ATT_W), row(ATT_W), row(ATT_W), row(ATT_W), row(3 * LANES), row(LANES)],
        out_shape=[out_bf(CONV_W), out_bf(ATT_W), out_bf(ATT_W), out_bf(ATT_W),
                   jax.ShapeDtypeStruct((b, s, ATT_W), F32), out_bf(3 * LANES),
                   jax.ShapeDtypeStruct((b, s, LANES), F32)],
        scratch_shapes=[pltpu.VMEM((tm + 8, CONV_W), F32)],
        compiler_params=_params(("parallel", "arbitrary")),
        name="even_in",
    )(x, g, w, wx, cw, gq, gk, lng, lnb, ca, sa, ci, si)


def _dsa_body(q_ref, k_ref, v_ref, qi_ref, kx_ref, wi_ref, o_ref, key_ref, hi_ref, lo_ref, *, tq, tk, topk):
    i = pl.program_id(1)
    lane = _lane_iota()
    lo = lane < HEAD
    hi = jnp.logical_not(lo)
    kf = float(topk)

    kx = kx_ref[0, 0:tk, :]
    wi = wi_ref[0]
    both = lambda t: t + pltpu.roll(t, HEAD, axis=1)
    score = None
    for h in range(N_HEADS):
        pair = qi_ref[0, :, (h // 2) * LANES:(h // 2 + 1) * LANES]
        q1, q2, q3 = (t.astype(F32) for t in _split3(jnp.where(lo if h % 2 == 0 else hi, pair, 0.0)))
        q13 = q1 + pltpu.roll(q3, HEAD, axis=1) if h % 2 == 0 else pltpu.roll(q1, HEAD, axis=1) + q3
        lhs = jnp.concatenate([both(q1), both(q2), q13], axis=-1).astype(BF16)
        t = jnp.maximum(_dot_nt(lhs, kx), 0.0) * wi[:, h:h + 1]
        score = t if score is None else score + t

    qpos = i * tq + lax.broadcasted_iota(I32, (tq, 1), 0)
    kpos = lax.broadcasted_iota(I32, (1, tk), 1)
    causal = kpos <= qpos
    score = jnp.where(score == 0.0, 0.0, score)
    bits = pltpu.bitcast(score, I32)
    key = jnp.where(bits < 0, bits ^ 0x7FFFFFFF, bits)
    key = jnp.where(causal, key, KEY_NEG_INF)
    key_ref[:, 0:tk] = key
    hi_ref[:, 0:tk] = lax.shift_right_arithmetic(key, 16).astype(I16)
    lo_ref[:, 0:tk] = ((key & 0xFFFF) - 32768).astype(I16)
    blocks = [slice(cb * LANES, (cb + 1) * LANES) for cb in range(tk // LANES)]

    def count16(ref, t, strict=False):
        t16 = t.astype(I16)
        acc = None
        for sl in blocks:
            m = ref[:, sl] > t16 if strict else ref[:, sl] >= t16
            one = jnp.where(m, jnp.int16(1), jnp.int16(0))
            acc = one if acc is None else acc + one
        return jnp.sum(acc.astype(F32), axis=-1, keepdims=True)

    def search16(ref, need):
        def bit_step(it, t):
            cand = t + lax.shift_left(jnp.int32(1), 15 - it)
            return jnp.where(count16(ref, cand) >= need, cand, t)
        return lax.fori_loop(0, 16, bit_step, jnp.full((tq, 1), -32768, I32))

    t_hi = search16(hi_ref, kf)
    above = count16(hi_ref, t_hi, strict=True)
    t_hi16 = t_hi.astype(I16)
    for sl in blocks:
        lo_ref[:, sl] = jnp.where(hi_ref[:, sl] == t_hi16, lo_ref[:, sl], jnp.int16(-32768))
    t_lo = search16(lo_ref, kf - above)
    thr = lax.shift_left(t_hi, 16) | (t_lo + 32768)

    excess = jnp.logical_and(above + count16(lo_ref, t_lo) > kf, thr > KEY_NEG_INF)
    any_excess = jnp.max(jnp.where(excess, 1.0, 0.0)) > 0.0

    @pl.when(any_excess)
    def _():
        need = kf - jnp.sum(jnp.where(key_ref[:, 0:tk] > thr, 1.0, 0.0), axis=-1, keepdims=True)
        r = lax.broadcasted_iota(I32, (LANES, LANES), 0)
        c = lax.broadcasted_iota(I32, (LANES, LANES), 1)
        upper = jnp.where(r <= c, 1.0, 0.0).astype(BF16)
        seen = jnp.zeros((tq, 1), F32)
        for cb in range(tk // LANES):
            kblk = key_ref[:, cb * LANES:(cb + 1) * LANES]
            tie = kblk == thr
            rank = _dot(jnp.where(tie, 1.0, 0.0).astype(BF16), upper) + seen
            drop = jnp.logical_and(tie, rank > need)
            key_ref[:, cb * LANES:(cb + 1) * LANES] = jnp.where(drop, INT_MIN, kblk)
            seen = rank[:, LANES - 1:LANES]

    sel = key_ref[:, 0:tk] >= jnp.maximum(thr, KEY_NEG_INF + 1)
    bias = jnp.where(sel, 0.0, NEG_BIG)

    outs = []
    for p in range(N_HEADS // 2):
        qp = q_ref[0, :, p * LANES:(p + 1) * LANES]
        kp = k_ref[0, 0:tk, p * LANES:(p + 1) * LANES]
        vp = v_ref[0, 0:tk, p * LANES:(p + 1) * LANES]
        halves = []
        for m, other in ((lo, HEAD), (hi, 0)):
            lg = _dot_nt(jnp.where(m, qp, jnp.zeros_like(qp)), kp) + bias
            e = jnp.exp((lg - jnp.max(lg, axis=-1, keepdims=True)).astype(BF16))
            ov = _dot(e, jnp.where(m, vp, jnp.ones_like(vp)))
            halves.append(ov / ov[:, other:other + 1])
        outs.append(jnp.where(lo, halves[0], halves[1]))
    o_ref[0] = jnp.concatenate(outs, axis=-1).astype(BF16)


def _dsa_kernel(q_ref, k_ref, v_ref, qi_ref, kx_ref, wi_ref, o_ref, key_ref, hi_ref, lo_ref, *, tq, var_w, n_var,
                topk):
    i = pl.program_id(1)
    per = var_w // tq
    for j in range(n_var):
        @pl.when(jnp.logical_and(i >= j * per, i < (j + 1) * per))
        def _():
            _dsa_body(q_ref, k_ref, v_ref, qi_ref, kx_ref, wi_ref, o_ref, key_ref, hi_ref, lo_ref,
                      tq=tq, tk=(j + 1) * var_w, topk=topk)


def _dsa(q, k, v, qi, kx, wi, *, tq):
    b, s, _ = q.shape
    topk = min(TOPK_MAX, s // 4)
    var_w = min(DSA_KEY_STEP, s)
    qrow = lambda width: pl.BlockSpec((1, tq, width), lambda i, j: (i, j, 0))
    full = lambda width: pl.BlockSpec((1, s, width), lambda i, j: (i, 0, 0))
    return pl.pallas_call(
        functools.partial(_dsa_kernel, tq=tq, var_w=var_w, n_var=s // var_w, topk=topk),
        grid=(b, s // tq),
        in_specs=[qrow(ATT_W), full(ATT_W), full(ATT_W), qrow(ATT_W), full(kx.shape[2]), qrow(LANES)],
        out_specs=qrow(ATT_W),
        out_shape=jax.ShapeDtypeStruct((b, s, ATT_W), BF16),
        scratch_shapes=[pltpu.VMEM((tq, s), I32), pltpu.VMEM((tq, s), I16), pltpu.VMEM((tq, s), I16)],
        compiler_params=_params(("parallel", "arbitrary")),
        name="dsa",
    )(q, k, v, qi, kx, wi)


def _post_kernel(*refs, n_y, ff_chunk):
    h_ref = refs[0]
    y_refs = refs[1:1 + n_y]
    (wo_ref, p_ref, gf_ref, wg_ref, wu_ref, wd_ref, wp_ref, gpost_ref, gple_ref, wgate_ref,
     o_ref, acc_ref) = refs[1 + n_y:]

    y = y_refs[0][...] if n_y == 1 else jnp.concatenate([r[...] for r in y_refs], axis=-1)
    h1 = h_ref[...] + _dot(y, wo_ref[...])
    f = _rms(h1, gf_ref[...]).astype(BF16)
    acc_ref[...] = h1
    d_ff = wg_ref.shape[1]
    for c in range(d_ff // ff_chunk):
        cs = slice(c * ff_chunk, (c + 1) * ff_chunk)
        gate = _dot(f, wg_ref[:, cs])
        act = (gate * jax.nn.sigmoid(gate) * _dot(f, wu_ref[:, cs])).astype(BF16)
        acc_ref[...] += _dot(act, wd_ref[cs, :])
    h2 = acc_ref[...]
    e = _rms(_dot(p_ref[...].astype(BF16), wp_ref[...]), gpost_ref[...])
    gate = jax.nn.sigmoid(_dot(_rms(h2, gple_ref[...]).astype(BF16), wgate_ref[...]))
    o_ref[...] = h2 + e * gate


def _post(h, ys, wo, p, gf, wg, wu, wd, wp, gpost, gple, wgate, *, tm, ff_chunk=256):
    r, d = h.shape
    row = lambda width: pl.BlockSpec((tm, width), lambda i: (i, 0))
    consts = [gf, wg, wu, wd, wp, gpost, gple, wgate]
    return pl.pallas_call(
        functools.partial(_post_kernel, n_y=len(ys), ff_chunk=ff_chunk),
        grid=(r // tm,),
        in_specs=([row(d)] + [row(y.shape[1]) for y in ys] + [_const_spec(wo.shape), row(p.shape[1])]
                  + [_const_spec(c.shape) for c in consts]),
        out_specs=row(d),
        out_shape=jax.ShapeDtypeStruct((r, d), F32),
        scratch_shapes=[pltpu.VMEM((tm, d), F32)],
        compiler_params=_params(("parallel",)),
        name="post",
    )(h, *ys, wo, p, *consts)


def _softplus(x):
    return jnp.maximum(x, 0.0) + jnp.log(1.0 + jnp.exp(-jnp.abs(x)))


def _odd_in_kernel(x_ref, g_ref, wqkv_ref, wz_ref, wab_ref, wabt_ref, cw_ref, acol_ref, dtcol_ref,
                   arow_ref, dtrow_ref,
                   q_ref, k_ref, v_ref, z_ref, gb_ref, gt_ref, ubuf_ref, *, tm):
    s = pl.program_id(1)
    hn = _rms(x_ref[0], g_ref[...]).astype(BF16)
    n_qk = 2 * DN_HEADS
    width = wqkv_ref.shape[1]

    @pl.when(s == 0)
    def _():
        ubuf_ref[0:8, :] = jnp.zeros((8, width), F32)

    @pl.when(s > 0)
    def _():
        ubuf_ref[0:8, :] = ubuf_ref[tm:tm + 8, :]

    ubuf_ref[8:tm + 8, :] = _dot(hn, wqkv_ref[...])
    for cb in range(width // LANES):
        cs = slice(cb * LANES, (cb + 1) * LANES)
        y = (cw_ref[0:1, cs] * ubuf_ref[5:tm + 5, cs] + cw_ref[1:2, cs] * ubuf_ref[6:tm + 6, cs]
             + cw_ref[2:3, cs] * ubuf_ref[7:tm + 7, cs] + cw_ref[3:4, cs] * ubuf_ref[8:tm + 8, cs])
        y = y * jax.nn.sigmoid(y)
        if cb < n_qk:
            y = y * lax.rsqrt(jnp.sum(y * y, axis=-1, keepdims=True) + EPS)
            if cb < DN_HEADS:
                q_ref[0, :, cs] = y * (DN_D ** -0.5)
            else:
                k_ref[0, :, slice((cb - DN_HEADS) * LANES, (cb - DN_HEADS + 1) * LANES)] = y
        else:
            v_ref[0, :, slice((cb - n_qk) * LANES, (cb - n_qk + 1) * LANES)] = y

    z_ref[0] = _dot(hn, wz_ref[...])

    ab = _dot(hn, wab_ref[...])
    lane = _lane_iota()
    g_tok = -jnp.exp(arow_ref[...]) * _softplus(ab + dtrow_ref[...])
    gb_ref[0] = jnp.where(lane < DN_HEADS, g_tok, jax.nn.sigmoid(ab))
    at = _dot_nt(wabt_ref[...], hn)
    gt_ref[0] = -jnp.exp(acol_ref[...]) * _softplus(at + dtcol_ref[...])


def _odd_in(x, g, wqkv, wz, wab, wabt, cw, acol, dtcol, arow, dtrow, *, tm):
    b, s, d = x.shape
    hd = DN_HEADS * DN_D
    row = lambda width: pl.BlockSpec((1, tm, width), lambda i, j: (i, j, 0))
    consts = [g, wqkv, wz, wab, wabt, cw, acol, dtcol, arow, dtrow]
    f32 = lambda *shape: jax.ShapeDtypeStruct(shape, F32)
    return pl.pallas_call(
        functools.partial(_odd_in_kernel, tm=tm),
        grid=(b, s // tm),
        in_specs=[row(d)] + [_const_spec(c.shape) for c in consts],
        out_specs=[row(hd), row(hd), row(hd), row(hd), row(LANES),
                   pl.BlockSpec((1, DN_HEADS, tm), lambda i, j: (i, 0, j))],
        out_shape=[f32(b, s, hd), f32(b, s, hd), f32(b, s, hd), f32(b, s, hd), f32(b, s, LANES),
                   f32(b, DN_HEADS, s)],
        scratch_shapes=[pltpu.VMEM((tm + 8, wqkv.shape[1]), F32)],
        compiler_params=_params(("parallel", "arbitrary")),
        name="odd_in",
    )(x, *consts)


def _delta_kernel(q_ref, k_ref, v_ref, z_ref, gb_ref, gt_ref, on_ref, y_ref, state_ref, *, n_chunk):
    c = DN_CHUNK

    @pl.when(pl.program_id(1) == 0)
    def _():
        state_ref[...] = jnp.zeros(state_ref.shape, F32)

    ri = lax.broadcasted_iota(I32, (c, c), 0)
    cj = lax.broadcasted_iota(I32, (c, c), 1)
    tril = ri >= cj
    strict = ri > cj
    eye = jnp.where(ri == cj, 1.0, 0.0)
    lower = jnp.where(tril, 1.0, 0.0).astype(BF16)
    upper = jnp.where(ri <= cj, 1.0, 0.0).astype(BF16)

    probs = [(n, h) for n in range(n_chunk) for h in range(DN_HEADS)]
    rows = lambda n: slice(n * c, (n + 1) * c)
    cols = lambda h: slice(h * DN_D, (h + 1) * DN_D)

    gbs, gc_col, gc_row, e_col, d_col, e_last = [], [], [], [], [], []
    for n in range(n_chunk):
        gb = gb_ref[0, rows(n), :]
        g1, g2, g3 = _split3(gb)
        gcc = _dot(lower, g1) + (_dot(lower, g2) + _dot(lower, g3))
        t1, t2, t3 = _split3(gt_ref[0, n])
        gcr = _dot(t1, upper) + (_dot(t2, upper) + _dot(t3, upper))
        g_last = gcc[c - 1:c, :]
        gbs.append(gb)
        gc_col.append(gcc)
        gc_row.append(gcr)
        e_col.append(jnp.exp(gcc))
        d_col.append(jnp.exp(g_last - gcc))
        e_last.append(jnp.exp(g_last))

    k16, kb, decay = {}, {}, {}
    for n, h in probs:
        kh = k_ref[0, rows(n), cols(h)]
        k16[n, h] = kh.astype(BF16)
        kb[n, h] = kh * gbs[n][:, DN_HEADS + h:DN_HEADS + h + 1]
        diff = gc_col[n][:, h:h + 1] - gc_row[n][h:h + 1, :]
        decay[n, h] = jnp.exp(jnp.where(tril, diff, NEG_BIG))
    a = {p: jnp.where(strict, _dot_nt(kb[p].astype(BF16), k16[p]) * decay[p], 0.0) for p in probs}
    qk16 = {}
    for n, h in probs:
        qh = q_ref[0, rows(n), cols(h)].astype(BF16)
        qk16[n, h] = jnp.where(tril, _dot_nt(qh, k16[n, h]) * decay[n, h], 0.0).astype(BF16)

    pw = {p: -a[p] for p in probs}
    tinv = {p: eye + pw[p] for p in probs}
    for _ in range(5):
        p16 = {p: pw[p].astype(BF16) for p in probs}
        pw = {p: _dot(p16[p], p16[p]) for p in probs}
        tinv = {p: tinv[p] + _dot(tinv[p].astype(BF16), pw[p].astype(BF16)) for p in probs}
    t16 = {p: tinv[p].astype(BF16) for p in probs}
    u, w16 = {}, {}
    for n, h in probs:
        beta = gbs[n][:, DN_HEADS + h:DN_HEADS + h + 1]
        u[n, h] = _dot(t16[n, h], (v_ref[0, rows(n), cols(h)] * beta).astype(BF16))
        w16[n, h] = _dot(t16[n, h], (kb[n, h] * e_col[n][:, h:h + 1]).astype(BF16)).astype(BF16)

    heads = range(DN_HEADS)
    for n in range(n_chunk):
        st = [state_ref[h] for h in heads]
        s16 = [x.astype(BF16) for x in st]
        v16 = [(u[n, h] - _dot(w16[n, h], s16[h])).astype(BF16) for h in heads]
        for h in heads:
            qg = (q_ref[0, rows(n), cols(h)] * e_col[n][:, h:h + 1]).astype(BF16)
            o = _dot(qg, s16[h]) + _dot(qk16[n, h], v16[h])
            k_dec = (k_ref[0, rows(n), cols(h)] * d_col[n][:, h:h + 1]).astype(BF16)
            state_ref[h] = st[h] * e_last[n][:, h:h + 1] + _dot_tn(k_dec, v16[h])
            zh = z_ref[0, rows(n), cols(h)]
            on = o * lax.rsqrt(jnp.mean(o * o, axis=-1, keepdims=True) + EPS) * on_ref[...]
            y_ref[0, rows(n), cols(h)] = (on * (zh * jax.nn.sigmoid(zh))).astype(BF16)


def _delta(q, k, v, z, gb, gt, on, *, tc):
    b, s, hd = q.shape
    n_chunk = tc // DN_CHUNK
    row = lambda width: pl.BlockSpec((1, tc, width), lambda i, j: (i, j, 0))
    return pl.pallas_call(
        functools.partial(_delta_kernel, n_chunk=n_chunk),
        grid=(b, s // tc),
        in_specs=[row(hd), row(hd), row(hd), row(hd), row(LANES),
                  pl.BlockSpec((1, n_chunk, DN_HEADS, DN_CHUNK), lambda i, j: (i, j, 0, 0)),
                  _const_spec(on.shape)],
        out_specs=row(hd),
        out_shape=jax.ShapeDtypeStruct((b, s, hd), BF16),
        scratch_shapes=[pltpu.VMEM((DN_HEADS, DN_D, DN_D), F32)],
        compiler_params=_params(("parallel", "arbitrary")),
        name="delta",
    )(q, k, v, z, gb, gt, on)


def _rope_tables(positions, dim):
    inv_freq = 1.0 / (ROPE_THETA ** (jnp.arange(0, dim, 2, dtype=F32) / dim))
    ang = positions.astype(F32)[..., None] * inv_freq
    return jnp.cos(ang), jnp.sin(ang)


def _row(v):
    return v.reshape(1, -1).astype(F32)


def kernel(x, p, positions, norm_mix_g, norm_ffn_g, ev_w_in, ev_conv_w, ev_q_norm_g, ev_k_norm_g, ev_ik_ln_g, ev_ik_ln_b, ev_w_out, od_w_in, od_conv_w, od_a_log, od_dt_bias, od_o_norm_g, od_w_out, ffn_w_gate, ffn_w_up, ffn_w_down, ple_w_proj, ple_post_norm_g, ple_norm_g, ple_w_gate):
    b, s, d = x.shape
    depth = p.shape[0]
    tm = min(512, s)

    cos_a, sin_a = _rope_tables(positions, HEAD)
    cos_i, sin_i = _rope_tables(positions, HEAD // 2)
    ca = jnp.tile(jnp.concatenate([cos_a, cos_a], -1), (1, 1, 2))
    sa = jnp.tile(jnp.concatenate([-sin_a, sin_a], -1), (1, 1, 2))
    ones = jnp.ones(cos_i.shape[:-1] + (HEAD // 2,), F32)
    ci = jnp.tile(jnp.concatenate([cos_i, cos_i, ones], -1), (1, 1, 2))
    si = jnp.tile(jnp.concatenate([-sin_i, sin_i, 0.0 * ones], -1), (1, 1, 2))

    h = x
    for i in range(depth):
        j = i // 2
        if i % 2 == 0:
            w = ev_w_in[j]
            n_main = 3 * CONV_W + 3 * ATT_W
            w_pad = jnp.zeros((d, LANES - HEAD - N_HEADS), F32)
            w1, w2, w3 = _split3(jnp.concatenate([w[:, n_main:].astype(F32), w_pad], axis=1))
            wx = jnp.concatenate([w1, w2, w1, w3, w1, w2], axis=0)
            pad64 = lambda v: jnp.concatenate([v, jnp.zeros((HEAD,), v.dtype)]).reshape(1, LANES).astype(F32)
            ya, q, k, v, qi, kx, wi = _even_in(
                h, _row(norm_mix_g[i]), w[:, :n_main].astype(BF16), wx, ev_conv_w[j].astype(F32),
                _row(jnp.tile(ev_q_norm_g[j], N_HEADS)), _row(jnp.tile(ev_k_norm_g[j], N_HEADS)),
                pad64(ev_ik_ln_g[j]), pad64(ev_ik_ln_b[j]), ca, sa, ci, si, tm=tm)
            yb = _dsa(q, k, v, qi, kx, wi, tq=min(DSA_TQ, s))
            wo = ev_w_out[j].astype(BF16)
            ys = [ya.reshape(b * s, CONV_W), yb.reshape(b * s, ATT_W)]
        else:
            w = od_w_in[j]
            n_qkv = 3 * DN_HEADS * DN_D
            n_z = DN_HEADS * DN_D
            w_ab = w[:, n_qkv + n_z:]
            wab = jnp.concatenate([w_ab, jnp.zeros((d, LANES - 2 * DN_HEADS), w.dtype)], axis=1).astype(BF16)
            wabt = w_ab[:, :DN_HEADS].T.astype(BF16)
            a_log = od_a_log[j].astype(F32)
            dt_b = od_dt_bias[j].astype(F32)
            pad_row = lambda v: jnp.concatenate([v, jnp.zeros((LANES - DN_HEADS,), F32)]).reshape(1, LANES)
            q, k, v, z, gb, gt = _odd_in(
                h, _row(norm_mix_g[i]), w[:, :n_qkv].astype(BF16), w[:, n_qkv:n_qkv + n_z].astype(BF16),
                wab, wabt, od_conv_w[j].astype(F32), a_log.reshape(DN_HEADS, 1), dt_b.reshape(DN_HEADS, 1),
                pad_row(a_log), pad_row(dt_b), tm=tm)
            gt = gt.reshape(b, DN_HEADS, s // DN_CHUNK, DN_CHUNK).transpose(0, 2, 1, 3)
            y = _delta(q, k, v, z, gb, gt, _row(od_o_norm_g[j]), tc=min(DELTA_TILE, s))
            ys = [y.reshape(b * s, DN_HEADS * DN_D)]
            wo = od_w_out[j].astype(BF16)
        h = _post(h.reshape(b * s, d), ys, wo, p[i].reshape(b * s, -1), _row(norm_ffn_g[i]),
                  ffn_w_gate[i].astype(BF16), ffn_w_up[i].astype(BF16), ffn_w_down[i].astype(BF16),
                  ple_w_proj[i].astype(BF16), _row(ple_post_norm_g[i]), _row(ple_norm_g[i]),
                  ple_w_gate[i].astype(BF16), tm=tm).reshape(b, s, d)
    return h
```

```python
import functools

import jax
import jax.numpy as jnp
from jax import lax
from jax.experimental import pallas as pl
from jax.experimental.pallas import tpu as pltpu

F32 = jnp.float32
BF16 = jnp.bfloat16
I32 = jnp.int32
I16 = jnp.int16

EPS = 1e-6
ROPE_THETA = 10000.0
LANES = 128
HEAD = 64
N_HEADS = 8
CONV_W = 512
ATT_W = 512
TOPK_MAX = 256
DSA_TQ = 128
DSA_KEY_STEP = 512
DN_HEADS = 8
DN_D = 128
DN_CHUNK = 64
DELTA_TILE = 256
VMEM_LIMIT = 56 * 1024 * 1024

KEY_NEG_INF = -2139095041
INT_MIN = -2147483648
NEG_BIG = -1e30

NT_DIMS = (((1,), (1,)), ((), ()))
TN_DIMS = (((0,), (0,)), ((), ()))


def _dot(a, b):
    return jnp.dot(a, b, preferred_element_type=F32)


def _dot_nt(a, b):
    return lax.dot_general(a, b, NT_DIMS, preferred_element_type=F32)


def _dot_tn(a, b):
    return lax.dot_general(a, b, TN_DIMS, preferred_element_type=F32)


def _rms(x, g):
    ms = jnp.mean(x * x, axis=-1, keepdims=True)
    return x * lax.rsqrt(ms + EPS) * g


def _split3(x):
    x1 = x.astype(BF16)
    r = x - x1.astype(F32)
    x2 = r.astype(BF16)
    x3 = (r - x2.astype(F32)).astype(BF16)
    return x1, x2, x3


def _const_spec(shape):
    nd = len(shape)
    return pl.BlockSpec(shape, lambda *_: (0,) * nd, pipeline_mode=pl.Buffered(1))


def _params(sem):
    return pltpu.CompilerParams(dimension_semantics=sem, vmem_limit_bytes=VMEM_LIMIT)


def _lane_iota():
    return lax.broadcasted_iota(I32, (1, LANES), 1)


def _swap_halves(x, half):
    lane = _lane_iota()
    first = (lane % (2 * half)) < half
    return jnp.where(first, pltpu.roll(x, LANES - half, axis=1), pltpu.roll(x, half, axis=1))


def _even_in_kernel(x_ref, g_ref, w_ref, wx_ref, cw_ref, gq_ref, gk_ref, lng_ref, lnb_ref,
                    ca_ref, sa_ref, ci_ref, si_ref,
                    ya_ref, q_ref, k_ref, v_ref, qi_ref, kx_ref, wi_ref, ubuf_ref, *, tm, wi_scale):
    s = pl.program_id(1)
    hn = _rms(x_ref[0], g_ref[...]).astype(BF16)
    lane = _lane_iota()
    lo = lane < HEAD

    u = _dot(hn, w_ref[:, 512:1024]) * _dot(hn, w_ref[:, 1024:1536])

    @pl.when(s == 0)
    def _():
        ubuf_ref[0:8, :] = jnp.zeros((8, CONV_W), F32)

    @pl.when(s > 0)
    def _():
        ubuf_ref[0:8, :] = ubuf_ref[tm:tm + 8, :]

    ubuf_ref[8:tm + 8, :] = u
    conv = (cw_ref[0:1, :] * ubuf_ref[6:tm + 6, :] + cw_ref[1:2, :] * ubuf_ref[7:tm + 7, :]
            + cw_ref[2:3, :] * u)
    ya_ref[0] = (_dot(hn, w_ref[:, 0:512]) * conv).astype(BF16)

    ca = ca_ref[0]
    sa = sa_ref[0]
    ci = ci_ref[0]
    si = si_ref[0]

    def head_norm_rope(z, gain, scale):
        outs = []
        for j in range(ATT_W // LANES):
            xb = z[:, j * LANES:(j + 1) * LANES]
            sq = xb * xb
            s_lo = jnp.sum(jnp.where(lo, sq, 0.0), axis=-1, keepdims=True)
            s_hi = jnp.sum(jnp.where(lo, 0.0, sq), axis=-1, keepdims=True)
            r = jnp.where(lo, lax.rsqrt(s_lo * (1.0 / HEAD) + EPS), lax.rsqrt(s_hi * (1.0 / HEAD) + EPS))
            xn = xb * r * gain[:, j * LANES:(j + 1) * LANES]
            outs.append((xn * ca + _swap_halves(xn, HEAD // 2) * sa) * scale)
        return jnp.concatenate(outs, axis=-1)

    q_ref[0] = head_norm_rope(_dot(hn, w_ref[:, 1536:2048]), gq_ref[...], HEAD ** -0.5).astype(BF16)
    k_ref[0] = head_norm_rope(_dot(hn, w_ref[:, 2048:2560]), gk_ref[...], 1.0).astype(BF16)
    v_ref[0] = _dot(hn, w_ref[:, 2560:3072]).astype(BF16)

    zidx = _dot(hn, wx_ref[...])
    zqi = zidx[:, 0:ATT_W]
    outs = []
    for j in range(ATT_W // LANES):
        xb = zqi[:, j * LANES:(j + 1) * LANES]
        outs.append(xb * ci + _swap_halves(xb, HEAD // 4) * si)
    qi_ref[0] = jnp.concatenate(outs, axis=-1).astype(BF16)

    zkw = zidx[:, ATT_W:ATT_W + LANES]
    mu = jnp.sum(jnp.where(lo, zkw, 0.0), axis=-1, keepdims=True) * (1.0 / HEAD)
    d = zkw - mu
    var = jnp.sum(jnp.where(lo, d * d, 0.0), axis=-1, keepdims=True) * (1.0 / HEAD)
    kn = d * lax.rsqrt(var + EPS) * lng_ref[...] + lnb_ref[...]
    kr = jnp.where(lo, kn * ci + _swap_halves(kn, HEAD // 4) * si, 0.0)
    kx_ref[0] = (kr + pltpu.roll(kr, HEAD, axis=1)).astype(BF16)
    wi_ref[0] = jnp.where(lane < N_HEADS, pltpu.roll(zkw, HEAD, axis=1) * wi_scale, 0.0)


def _even_in(x, g, w, wx, cw, gq, gk, lng, lnb, ca, sa, ci, si, *, tm):
    b, s, d = x.shape
    row = lambda width: pl.BlockSpec((1, tm, width), lambda i, j: (i, j, 0))
    out_bf = lambda width: jax.ShapeDtypeStruct((b, s, width), BF16)
    return pl.pallas_call(
        functools.partial(_even_in_kernel, tm=tm, wi_scale=N_HEADS ** -0.5 * HEAD ** -0.5),
        grid=(b, s // tm),
        in_specs=[row(d), _const_spec(g.shape), _const_spec(w.shape), _const_spec(wx.shape), _const_spec(cw.shape),
                  _const_spec(gq.shape), _const_spec(gk.shape), _const_spec(lng.shape), _const_spec(lnb.shape),
                  row(LANES), row(LANES), row(LANES), row(LANES)],
        out_specs=[row(CONV_W), row(ATT_W), row(ATT_W), row(ATT_W), row(ATT_W), row(LANES), row(LANES)],
        out_shape=[out_bf(CONV_W), out_bf(ATT_W), out_bf(ATT_W), out_bf(ATT_W), out_bf(ATT_W), out_bf(LANES),
                   jax.ShapeDtypeStruct((b, s, LANES), F32)],
        scratch_shapes=[pltpu.VMEM((tm + 8, CONV_W), F32)],
        compiler_params=_params(("parallel", "arbitrary")),
        name="even_in",
    )(x, g, w, wx, cw, gq, gk, lng, lnb, ca, sa, ci, si)


def _dsa_body(q_ref, k_ref, v_ref, qi_ref, kx_ref, wi_ref, shift_ref, o_ref, key_ref, hi_ref, lo_ref, *,
              tq, tk, topk):
    i = pl.program_id(1)
    lane = _lane_iota()
    lo = lane < HEAD
    hi = jnp.logical_not(lo)
    kf = float(topk)

    kx = kx_ref[0, 0:tk, :]
    wi = wi_ref[0]
    score = None
    for h in range(N_HEADS):
        pair = qi_ref[0, :, (h // 2) * LANES:(h // 2 + 1) * LANES]
        qh = jnp.where(lo if h % 2 == 0 else hi, pair, jnp.zeros_like(pair))
        t = jnp.maximum(_dot_nt(qh, kx), 0.0) * wi[:, h:h + 1]
        score = t if score is None else score + t

    qpos = i * tq + lax.broadcasted_iota(I32, (tq, 1), 0)
    kpos = lax.broadcasted_iota(I32, (1, tk), 1)
    causal = kpos <= qpos
    score = jnp.where(score == 0.0, 0.0, score)
    bits = pltpu.bitcast(score, I32)
    key = jnp.where(bits < 0, bits ^ 0x7FFFFFFF, bits)
    key = jnp.where(causal, key, KEY_NEG_INF)
    key_ref[:, 0:tk] = key
    hi_ref[:, 0:tk] = lax.shift_right_arithmetic(key, 16).astype(I16)
    lo_ref[:, 0:tk] = ((key & 0xFFFF) - 32768).astype(I16)
    blocks = [slice(cb * LANES, (cb + 1) * LANES) for cb in range(tk // LANES)]

    def count16(ref, t, strict=False):
        t16 = t.astype(I16)
        acc = None
        for sl in blocks:
            m = ref[:, sl] > t16 if strict else ref[:, sl] >= t16
            one = jnp.where(m, jnp.int16(1), jnp.int16(0))
            acc = one if acc is None else acc + one
        return jnp.sum(acc.astype(F32), axis=-1, keepdims=True)

    def search16(ref, need):
        def bit_step(it, t):
            cand = t + lax.shift_left(jnp.int32(1), 15 - it)
            return jnp.where(count16(ref, cand) >= need, cand, t)
        return lax.fori_loop(0, 16, bit_step, jnp.full((tq, 1), -32768, I32))

    t_hi = search16(hi_ref, kf)
    above = count16(hi_ref, t_hi, strict=True)
    t_hi16 = t_hi.astype(I16)
    for sl in blocks:
        lo_ref[:, sl] = jnp.where(hi_ref[:, sl] == t_hi16, lo_ref[:, sl], jnp.int16(-32768))
    t_lo = search16(lo_ref, kf - above)
    thr = lax.shift_left(t_hi, 16) | (t_lo + 32768)

    excess = jnp.logical_and(above + count16(lo_ref, t_lo) > kf, thr > KEY_NEG_INF)
    any_excess = jnp.max(jnp.where(excess, 1.0, 0.0)) > 0.0

    @pl.when(any_excess)
    def _():
        need = kf - jnp.sum(jnp.where(key_ref[:, 0:tk] > thr, 1.0, 0.0), axis=-1, keepdims=True)
        r = lax.broadcasted_iota(I32, (LANES, LANES), 0)
        c = lax.broadcasted_iota(I32, (LANES, LANES), 1)
        upper = jnp.where(r <= c, 1.0, 0.0).astype(BF16)
        seen = jnp.zeros((tq, 1), F32)
        for cb in range(tk // LANES):
            kblk = key_ref[:, cb * LANES:(cb + 1) * LANES]
            tie = kblk == thr
            rank = _dot(jnp.where(tie, 1.0, 0.0).astype(BF16), upper) + seen
            drop = jnp.logical_and(tie, rank > need)
            key_ref[:, cb * LANES:(cb + 1) * LANES] = jnp.where(drop, INT_MIN, kblk)
            seen = rank[:, LANES - 1:LANES]

    sel = key_ref[:, 0:tk] >= jnp.maximum(thr, KEY_NEG_INF + 1)
    selb = jnp.where(sel, 1.0, 0.0).astype(BF16)

    shift = shift_ref[...]
    outs, l_min = [], None
    for p in range(N_HEADS // 2):
        qp = q_ref[0, :, p * LANES:(p + 1) * LANES].astype(F32)
        kp = k_ref[0, 0:tk, p * LANES:(p + 1) * LANES]
        vp = v_ref[0, 0:tk, p * LANES:(p + 1) * LANES]
        halves = []
        for m, other in ((lo, HEAD), (hi, 0)):
            spare = lane == other
            qa = jnp.where(m, qp, jnp.where(spare, -shift, 0.0)).astype(BF16)
            ka = jnp.where(m, kp, jnp.where(spare, 1.0, 0.0).astype(BF16))
            e = jnp.exp(_dot_nt(qa, ka)).astype(BF16) * selb
            ov = _dot(e, jnp.where(m, vp, jnp.ones_like(vp)))
            l = ov[:, other:other + 1]
            l_min = l if l_min is None else jnp.minimum(l_min, l)
            halves.append(ov / l)
        outs.append(jnp.where(lo, halves[0], halves[1]))
    o_ref[0] = jnp.concatenate(outs, axis=-1).astype(BF16)

    @pl.when(jnp.min(l_min) < 1e-30)
    def _():
        bias = jnp.where(sel, 0.0, NEG_BIG)
        outs = []
        for p in range(N_HEADS // 2):
            qp = q_ref[0, :, p * LANES:(p + 1) * LANES]
            kp = k_ref[0, 0:tk, p * LANES:(p + 1) * LANES]
            vp = v_ref[0, 0:tk, p * LANES:(p + 1) * LANES]
            halves = []
            for m, other in ((lo, HEAD), (hi, 0)):
                lg = _dot_nt(jnp.where(m, qp, jnp.zeros_like(qp)), kp) + bias
                e = jnp.exp(lg - jnp.max(lg, axis=-1, keepdims=True)).astype(BF16)
                ov = _dot(e, jnp.where(m, vp, jnp.ones_like(vp)))
                halves.append(ov / ov[:, other:other + 1])
            outs.append(jnp.where(lo, halves[0], halves[1]))
        o_ref[0] = jnp.concatenate(outs, axis=-1).astype(BF16)


def _dsa_kernel(*refs, tq, var_w, n_var, topk):
    i = pl.program_id(1)
    per = var_w // tq
    for j in range(n_var):
        @pl.when(jnp.logical_and(i >= j * per, i < (j + 1) * per))
        def _():
            _dsa_body(*refs, tq=tq, tk=(j + 1) * var_w, topk=topk)


def _dsa(q, k, v, qi, kx, wi, shift, *, tq):
    b, s, _ = q.shape
    topk = min(TOPK_MAX, s // 4)
    var_w = min(DSA_KEY_STEP, s)
    qrow = lambda width: pl.BlockSpec((1, tq, width), lambda i, j: (i, j, 0))
    full = lambda width: pl.BlockSpec((1, s, width), lambda i, j: (i, 0, 0))
    return pl.pallas_call(
        functools.partial(_dsa_kernel, tq=tq, var_w=var_w, n_var=s // var_w, topk=topk),
        grid=(b, s // tq),
        in_specs=[qrow(ATT_W), full(ATT_W), full(ATT_W), qrow(ATT_W), full(kx.shape[2]), qrow(LANES),
                  _const_spec(shift.shape)],
        out_specs=qrow(ATT_W),
        out_shape=jax.ShapeDtypeStruct((b, s, ATT_W), BF16),
        scratch_shapes=[pltpu.VMEM((tq, s), I32), pltpu.VMEM((tq, s), I16), pltpu.VMEM((tq, s), I16)],
        compiler_params=_params(("parallel", "arbitrary")),
        name="dsa",
    )(q, k, v, qi, kx, wi, shift)


def _post_kernel(*refs, n_y, ff_chunk):
    h_ref = refs[0]
    y_refs = refs[1:1 + n_y]
    (wo_ref, p_ref, gf_ref, wg_ref, wu_ref, wd_ref, wp_ref, gpost_ref, gple_ref, wgate_ref,
     o_ref, acc_ref) = refs[1 + n_y:]

    y = y_refs[0][...] if n_y == 1 else jnp.concatenate([r[...] for r in y_refs], axis=-1)
    h1 = h_ref[...] + _dot(y, wo_ref[...])
    f = _rms(h1, gf_ref[...]).astype(BF16)
    acc_ref[...] = h1
    d_ff = wg_ref.shape[1]
    for c in range(d_ff // ff_chunk):
        cs = slice(c * ff_chunk, (c + 1) * ff_chunk)
        gate = _dot(f, wg_ref[:, cs])
        act = (gate * jax.nn.sigmoid(gate) * _dot(f, wu_ref[:, cs])).astype(BF16)
        acc_ref[...] += _dot(act, wd_ref[cs, :])
    h2 = acc_ref[...]
    e = _rms(_dot(p_ref[...].astype(BF16), wp_ref[...]), gpost_ref[...])
    gate = jax.nn.sigmoid(_dot(_rms(h2, gple_ref[...]).astype(BF16), wgate_ref[...]))
    o_ref[...] = h2 + e * gate


def _post(h, ys, wo, p, gf, wg, wu, wd, wp, gpost, gple, wgate, *, tm, ff_chunk=256):
    r, d = h.shape
    row = lambda width: pl.BlockSpec((tm, width), lambda i: (i, 0))
    consts = [gf, wg, wu, wd, wp, gpost, gple, wgate]
    return pl.pallas_call(
        functools.partial(_post_kernel, n_y=len(ys), ff_chunk=ff_chunk),
        grid=(r // tm,),
        in_specs=([row(d)] + [row(y.shape[1]) for y in ys] + [_const_spec(wo.shape), row(p.shape[1])]
                  + [_const_spec(c.shape) for c in consts]),
        out_specs=row(d),
        out_shape=jax.ShapeDtypeStruct((r, d), F32),
        scratch_shapes=[pltpu.VMEM((tm, d), F32)],
        compiler_params=_params(("parallel",)),
        name="post",
    )(h, *ys, wo, p, *consts)


def _softplus(x):
    return jnp.maximum(x, 0.0) + jnp.log(1.0 + jnp.exp(-jnp.abs(x)))


def _odd_in_kernel(x_ref, g_ref, wqkv_ref, wz_ref, wab_ref, wabt_ref, cw_ref, acol_ref, dtcol_ref,
                   arow_ref, dtrow_ref,
                   q_ref, k_ref, v_ref, z_ref, gb_ref, gt_ref, ubuf_ref, *, tm):
    s = pl.program_id(1)
    hn = _rms(x_ref[0], g_ref[...]).astype(BF16)
    n_qk = 2 * DN_HEADS
    width = wqkv_ref.shape[1]

    @pl.when(s == 0)
    def _():
        ubuf_ref[0:8, :] = jnp.zeros((8, width), F32)

    @pl.when(s > 0)
    def _():
        ubuf_ref[0:8, :] = ubuf_ref[tm:tm + 8, :]

    ubuf_ref[8:tm + 8, :] = _dot(hn, wqkv_ref[...])
    for cb in range(width // LANES):
        cs = slice(cb * LANES, (cb + 1) * LANES)
        y = (cw_ref[0:1, cs] * ubuf_ref[5:tm + 5, cs] + cw_ref[1:2, cs] * ubuf_ref[6:tm + 6, cs]
             + cw_ref[2:3, cs] * ubuf_ref[7:tm + 7, cs] + cw_ref[3:4, cs] * ubuf_ref[8:tm + 8, cs])
        y = y * jax.nn.sigmoid(y)
        if cb < n_qk:
            y = y * lax.rsqrt(jnp.sum(y * y, axis=-1, keepdims=True) + EPS)
            if cb < DN_HEADS:
                q_ref[0, :, cs] = (y * (DN_D ** -0.5)).astype(BF16)
            else:
                k_ref[0, :, slice((cb - DN_HEADS) * LANES, (cb - DN_HEADS + 1) * LANES)] = y.astype(BF16)
        else:
            v_ref[0, :, slice((cb - n_qk) * LANES, (cb - n_qk + 1) * LANES)] = y.astype(BF16)

    z_ref[0] = _dot(hn, wz_ref[...]).astype(BF16)

    ab = _dot(hn, wab_ref[...])
    lane = _lane_iota()
    g_tok = -jnp.exp(arow_ref[...]) * _softplus(ab + dtrow_ref[...])
    gb_ref[0] = jnp.where(lane < DN_HEADS, g_tok, jax.nn.sigmoid(ab))
    at = _dot_nt(wabt_ref[...], hn)
    gt_ref[0] = -jnp.exp(acol_ref[...]) * _softplus(at + dtcol_ref[...])


def _odd_in(x, g, wqkv, wz, wab, wabt, cw, acol, dtcol, arow, dtrow, *, tm):
    b, s, d = x.shape
    hd = DN_HEADS * DN_D
    row = lambda width: pl.BlockSpec((1, tm, width), lambda i, j: (i, j, 0))
    consts = [g, wqkv, wz, wab, wabt, cw, acol, dtcol, arow, dtrow]
    f32 = lambda *shape: jax.ShapeDtypeStruct(shape, F32)
    bf16 = lambda *shape: jax.ShapeDtypeStruct(shape, BF16)
    return pl.pallas_call(
        functools.partial(_odd_in_kernel, tm=tm),
        grid=(b, s // tm),
        in_specs=[row(d)] + [_const_spec(c.shape) for c in consts],
        out_specs=[row(hd), row(hd), row(hd), row(hd), row(LANES),
                   pl.BlockSpec((1, DN_HEADS, tm), lambda i, j: (i, 0, j))],
        out_shape=[bf16(b, s, hd), bf16(b, s, hd), bf16(b, s, hd), bf16(b, s, hd), f32(b, s, LANES),
                   f32(b, DN_HEADS, s)],
        scratch_shapes=[pltpu.VMEM((tm + 8, wqkv.shape[1]), F32)],
        compiler_params=_params(("parallel", "arbitrary")),
        name="odd_in",
    )(x, *consts)


def _delta_kernel(q_ref, k_ref, v_ref, z_ref, gb_ref, gt_ref, on_ref, y_ref, state_ref, *, n_chunk):
    c = DN_CHUNK

    @pl.when(pl.program_id(1) == 0)
    def _():
        state_ref[...] = jnp.zeros(state_ref.shape, F32)

    ri = lax.broadcasted_iota(I32, (c, c), 0)
    cj = lax.broadcasted_iota(I32, (c, c), 1)
    tril = ri >= cj
    strict = ri > cj
    eye = jnp.where(ri == cj, 1.0, 0.0)
    lower = jnp.where(tril, 1.0, 0.0).astype(BF16)
    upper = jnp.where(ri <= cj, 1.0, 0.0).astype(BF16)

    probs = [(n, h) for n in range(n_chunk) for h in range(DN_HEADS)]
    rows = lambda n: slice(n * c, (n + 1) * c)
    cols = lambda h: slice(h * DN_D, (h + 1) * DN_D)

    gbs, gc_col, gc_row, e_col, d_col, e_last = [], [], [], [], [], []
    for n in range(n_chunk):
        gb = gb_ref[0, rows(n), :]
        g1, g2, g3 = _split3(gb)
        gcc = _dot(lower, g1) + (_dot(lower, g2) + _dot(lower, g3))
        t1, t2, t3 = _split3(gt_ref[0, n])
        gcr = _dot(t1, upper) + (_dot(t2, upper) + _dot(t3, upper))
        g_last = gcc[c - 1:c, :]
        gbs.append(gb)
        gc_col.append(gcc)
        gc_row.append(gcr)
        e_col.append(jnp.exp(gcc))
        d_col.append(jnp.exp(g_last - gcc))
        e_last.append(jnp.exp(g_last))

    k16, kb, decay = {}, {}, {}
    for n, h in probs:
        k16[n, h] = k_ref[0, rows(n), cols(h)]
        kb[n, h] = k16[n, h].astype(F32) * gbs[n][:, DN_HEADS + h:DN_HEADS + h + 1]
        diff = gc_col[n][:, h:h + 1] - gc_row[n][h:h + 1, :]
        decay[n, h] = jnp.exp(jnp.where(tril, diff, NEG_BIG))
    a = {p: jnp.where(strict, _dot_nt(kb[p].astype(BF16), k16[p]) * decay[p], 0.0) for p in probs}
    qk16 = {}
    for n, h in probs:
        qk16[n, h] = jnp.where(tril, _dot_nt(q_ref[0, rows(n), cols(h)], k16[n, h]) * decay[n, h],
                               0.0).astype(BF16)

    pw = {p: -a[p] for p in probs}
    tinv = {p: eye + pw[p] for p in probs}
    for _ in range(5):
        p16 = {p: pw[p].astype(BF16) for p in probs}
        pw = {p: _dot(p16[p], p16[p]) for p in probs}
        tinv = {p: tinv[p] + _dot(tinv[p].astype(BF16), pw[p].astype(BF16)) for p in probs}
    t16 = {p: tinv[p].astype(BF16) for p in probs}
    u, w16 = {}, {}
    for n, h in probs:
        beta = gbs[n][:, DN_HEADS + h:DN_HEADS + h + 1]
        u[n, h] = _dot(t16[n, h], (v_ref[0, rows(n), cols(h)].astype(F32) * beta).astype(BF16))
        w16[n, h] = _dot(t16[n, h], (kb[n, h] * e_col[n][:, h:h + 1]).astype(BF16)).astype(BF16)

    heads = range(DN_HEADS)
    for n in range(n_chunk):
        st = [state_ref[h] for h in heads]
        s16 = [x.astype(BF16) for x in st]
        v16 = [(u[n, h] - _dot(w16[n, h], s16[h])).astype(BF16) for h in heads]
        for h in heads:
            qg = (q_ref[0, rows(n), cols(h)].astype(F32) * e_col[n][:, h:h + 1]).astype(BF16)
            o = _dot(qg, s16[h]) + _dot(qk16[n, h], v16[h])
            k_dec = (k16[n, h].astype(F32) * d_col[n][:, h:h + 1]).astype(BF16)
            state_ref[h] = st[h] * e_last[n][:, h:h + 1] + _dot_tn(k_dec, v16[h])
            zh = z_ref[0, rows(n), cols(h)].astype(F32)
            on = o * lax.rsqrt(jnp.mean(o * o, axis=-1, keepdims=True) + EPS) * on_ref[...]
            y_ref[0, rows(n), cols(h)] = (on * (zh * jax.nn.sigmoid(zh))).astype(BF16)


def _delta(q, k, v, z, gb, gt, on, *, tc):
    b, s, hd = q.shape
    n_chunk = tc // DN_CHUNK
    row = lambda width: pl.BlockSpec((1, tc, width), lambda i, j: (i, j, 0))
    return pl.pallas_call(
        functools.partial(_delta_kernel, n_chunk=n_chunk),
        grid=(b, s // tc),
        in_specs=[row(hd), row(hd), row(hd), row(hd), row(LANES),
                  pl.BlockSpec((1, n_chunk, DN_HEADS, DN_CHUNK), lambda i, j: (i, j, 0, 0)),
                  _const_spec(on.shape)],
        out_specs=row(hd),
        out_shape=jax.ShapeDtypeStruct((b, s, hd), BF16),
        scratch_shapes=[pltpu.VMEM((DN_HEADS, DN_D, DN_D), F32)],
        compiler_params=_params(("parallel", "arbitrary")),
        name="delta",
    )(q, k, v, z, gb, gt, on)


def _rope_tables(positions, dim):
    inv_freq = 1.0 / (ROPE_THETA ** (jnp.arange(0, dim, 2, dtype=F32) / dim))
    ang = positions.astype(F32)[..., None] * inv_freq
    return jnp.cos(ang), jnp.sin(ang)


def _row(v):
    return v.reshape(1, -1).astype(F32)


def kernel(x, p, positions, norm_mix_g, norm_ffn_g, ev_w_in, ev_conv_w, ev_q_norm_g, ev_k_norm_g, ev_ik_ln_g, ev_ik_ln_b, ev_w_out, od_w_in, od_conv_w, od_a_log, od_dt_bias, od_o_norm_g, od_w_out, ffn_w_gate, ffn_w_up, ffn_w_down, ple_w_proj, ple_post_norm_g, ple_norm_g, ple_w_gate):
    b, s, d = x.shape
    depth = p.shape[0]
    tm = min(512, s)

    cos_a, sin_a = _rope_tables(positions, HEAD)
    cos_i, sin_i = _rope_tables(positions, HEAD // 2)
    ca = jnp.tile(jnp.concatenate([cos_a, cos_a], -1), (1, 1, 2))
    sa = jnp.tile(jnp.concatenate([-sin_a, sin_a], -1), (1, 1, 2))
    ones = jnp.ones(cos_i.shape[:-1] + (HEAD // 2,), F32)
    ci = jnp.tile(jnp.concatenate([cos_i, cos_i, ones], -1), (1, 1, 2))
    si = jnp.tile(jnp.concatenate([-sin_i, sin_i, 0.0 * ones], -1), (1, 1, 2))

    h = x
    for i in range(depth):
        j = i // 2
        if i % 2 == 0:
            w = ev_w_in[j]
            n_main = 3 * CONV_W + 3 * ATT_W
            w_pad = jnp.zeros((d, LANES - HEAD - N_HEADS), w.dtype)
            wx = jnp.concatenate([w[:, n_main:], w_pad], axis=1).astype(BF16)
            pad64 = lambda v: jnp.concatenate([v, jnp.zeros((HEAD,), v.dtype)]).reshape(1, LANES).astype(F32)
            ya, q, k, v, qi, kx, wi = _even_in(
                h, _row(norm_mix_g[i]), w[:, :n_main].astype(BF16), wx, ev_conv_w[j].astype(F32),
                _row(jnp.tile(ev_q_norm_g[j], N_HEADS)), _row(jnp.tile(ev_k_norm_g[j], N_HEADS)),
                pad64(ev_ik_ln_g[j]), pad64(ev_ik_ln_b[j]), ca, sa, ci, si, tm=tm)
            shift = (HEAD ** 0.5 * jnp.max(jnp.abs(ev_q_norm_g[j])) * jnp.max(jnp.abs(ev_k_norm_g[j]))).astype(F32)
            yb = _dsa(q, k, v, qi, kx, wi, shift.reshape(1, 1), tq=min(DSA_TQ, s))
            wo = ev_w_out[j].astype(BF16)
            ys = [ya.reshape(b * s, CONV_W), yb.reshape(b * s, ATT_W)]
        else:
            w = od_w_in[j]
            n_qkv = 3 * DN_HEADS * DN_D
            n_z = DN_HEADS * DN_D
            w_ab = w[:, n_qkv + n_z:]
            wab = jnp.concatenate([w_ab, jnp.zeros((d, LANES - 2 * DN_HEADS), w.dtype)], axis=1).astype(BF16)
            wabt = w_ab[:, :DN_HEADS].T.astype(BF16)
            a_log = od_a_log[j].astype(F32)
            dt_b = od_dt_bias[j].astype(F32)
            pad_row = lambda v: jnp.concatenate([v, jnp.zeros((LANES - DN_HEADS,), F32)]).reshape(1, LANES)
            q, k, v, z, gb, gt = _odd_in(
                h, _row(norm_mix_g[i]), w[:, :n_qkv].astype(BF16), w[:, n_qkv:n_qkv + n_z].astype(BF16),
                wab, wabt, od_conv_w[j].astype(F32), a_log.reshape(DN_HEADS, 1), dt_b.reshape(DN_HEADS, 1),
                pad_row(a_log), pad_row(dt_b), tm=tm)
            gt = gt.reshape(b, DN_HEADS, s // DN_CHUNK, DN_CHUNK).transpose(0, 2, 1, 3)
            y = _delta(q, k, v, z, gb, gt, _row(od_o_norm_g[j]), tc=min(DELTA_TILE, s))
            ys = [y.reshape(b * s, DN_HEADS * DN_D)]
            wo = od_w_out[j].astype(BF16)
        h = _post(h.reshape(b * s, d), ys, wo, p[i].reshape(b * s, -1), _row(norm_ffn_g[i]),
                  ffn_w_gate[i].astype(BF16), ffn_w_up[i].astype(BF16), ffn_w_down[i].astype(BF16),
                  ple_w_proj[i].astype(BF16), _row(ple_post_norm_g[i]), _row(ple_norm_g[i]),
                  ple_w_gate[i].astype(BF16), tm=tm).reshape(b, s, d)
    return h
```

```python
import functools

import jax
import jax.numpy as jnp
from jax import lax
from jax.experimental import pallas as pl
from jax.experimental.pallas import tpu as pltpu

F32 = jnp.float32
BF16 = jnp.bfloat16
I32 = jnp.int32
I16 = jnp.int16

EPS = 1e-6
ROPE_THETA = 10000.0
LANES = 128
HEAD = 64
N_HEADS = 8
CONV_W = 512
ATT_W = 512
TOPK_MAX = 256
DSA_TQ = 128
DSA_KEY_STEP = 512
DN_HEADS = 8
DN_D = 128
DN_CHUNK = 64
DELTA_TILE = 256
VMEM_LIMIT = 56 * 1024 * 1024

KEY_NEG_INF = -2139095041
INT_MIN = -2147483648
NEG_BIG = -1e30
EXP_SAFE_LOGIT = 60.0

NT_DIMS = (((1,), (1,)), ((), ()))
TN_DIMS = (((0,), (0,)), ((), ()))


def _dot(a, b):
    return jnp.dot(a, b, preferred_element_type=F32)


def _dot_nt(a, b):
    return lax.dot_general(a, b, NT_DIMS, preferred_element_type=F32)


def _dot_tn(a, b):
    return lax.dot_general(a, b, TN_DIMS, preferred_element_type=F32)


def _rms(x, g):
    ms = jnp.mean(x * x, axis=-1, keepdims=True)
    return x * lax.rsqrt(ms + EPS) * g


def _split3(x):
    x1 = x.astype(BF16)
    r = x - x1.astype(F32)
    x2 = r.astype(BF16)
    x3 = (r - x2.astype(F32)).astype(BF16)
    return x1, x2, x3


def _const_spec(shape):
    nd = len(shape)
    return pl.BlockSpec(shape, lambda *_: (0,) * nd, pipeline_mode=pl.Buffered(1))


def _params(sem):
    return pltpu.CompilerParams(dimension_semantics=sem, vmem_limit_bytes=VMEM_LIMIT)


def _lane_iota():
    return lax.broadcasted_iota(I32, (1, LANES), 1)


def _swap_halves(x, half):
    lane = _lane_iota()
    first = (lane % (2 * half)) < half
    return jnp.where(first, pltpu.roll(x, LANES - half, axis=1), pltpu.roll(x, half, axis=1))


def _even_in_kernel(x_ref, g_ref, w_ref, wx_ref, cw_ref, gq_ref, gk_ref, lng_ref, lnb_ref,
                    ca_ref, sa_ref, ci_ref, si_ref,
                    ya_ref, q_ref, k_ref, v_ref, qi_ref, kx_ref, wi_ref, ubuf_ref, *, tm, wi_scale):
    s = pl.program_id(1)
    hn = _rms(x_ref[0], g_ref[...]).astype(BF16)
    lane = _lane_iota()
    lo = lane < HEAD

    u = _dot(hn, w_ref[:, 512:1024]) * _dot(hn, w_ref[:, 1024:1536])

    @pl.when(s == 0)
    def _():
        ubuf_ref[0:8, :] = jnp.zeros((8, CONV_W), F32)

    @pl.when(s > 0)
    def _():
        ubuf_ref[0:8, :] = ubuf_ref[tm:tm + 8, :]

    ubuf_ref[8:tm + 8, :] = u
    conv = (cw_ref[0:1, :] * ubuf_ref[6:tm + 6, :] + cw_ref[1:2, :] * ubuf_ref[7:tm + 7, :]
            + cw_ref[2:3, :] * u)
    ya_ref[0] = (_dot(hn, w_ref[:, 0:512]) * conv).astype(BF16)

    ca = ca_ref[0]
    sa = sa_ref[0]
    ci = ci_ref[0]
    si = si_ref[0]

    def head_norm_rope(z, gain, scale):
        outs = []
        for j in range(ATT_W // LANES):
            xb = z[:, j * LANES:(j + 1) * LANES]
            sq = xb * xb
            s_lo = jnp.sum(jnp.where(lo, sq, 0.0), axis=-1, keepdims=True)
            s_hi = jnp.sum(jnp.where(lo, 0.0, sq), axis=-1, keepdims=True)
            r = jnp.where(lo, lax.rsqrt(s_lo * (1.0 / HEAD) + EPS), lax.rsqrt(s_hi * (1.0 / HEAD) + EPS))
            xn = xb * r * gain[:, j * LANES:(j + 1) * LANES]
            outs.append((xn * ca + _swap_halves(xn, HEAD // 2) * sa) * scale)
        return jnp.concatenate(outs, axis=-1)

    q_ref[0] = head_norm_rope(_dot(hn, w_ref[:, 1536:2048]), gq_ref[...], HEAD ** -0.5).astype(BF16)
    k_ref[0] = head_norm_rope(_dot(hn, w_ref[:, 2048:2560]), gk_ref[...], 1.0).astype(BF16)
    v_ref[0] = _dot(hn, w_ref[:, 2560:3072]).astype(BF16)

    zidx = _dot(hn, wx_ref[...])
    zqi = zidx[:, 0:ATT_W]
    outs = []
    for j in range(ATT_W // LANES):
        xb = zqi[:, j * LANES:(j + 1) * LANES]
        outs.append(xb * ci + _swap_halves(xb, HEAD // 4) * si)
    qi_ref[0] = jnp.concatenate(outs, axis=-1).astype(BF16)

    zkw = zidx[:, ATT_W:ATT_W + LANES]
    mu = jnp.sum(jnp.where(lo, zkw, 0.0), axis=-1, keepdims=True) * (1.0 / HEAD)
    d = zkw - mu
    var = jnp.sum(jnp.where(lo, d * d, 0.0), axis=-1, keepdims=True) * (1.0 / HEAD)
    kn = d * lax.rsqrt(var + EPS) * lng_ref[...] + lnb_ref[...]
    kr = jnp.where(lo, kn * ci + _swap_halves(kn, HEAD // 4) * si, 0.0)
    kx_ref[0] = (kr + pltpu.roll(kr, HEAD, axis=1)).astype(BF16)
    wi_ref[0] = jnp.where(lane < N_HEADS, pltpu.roll(zkw, HEAD, axis=1) * wi_scale, 0.0)


def _even_in(x, g, w, wx, cw, gq, gk, lng, lnb, ca, sa, ci, si, *, tm):
    b, s, d = x.shape
    row = lambda width: pl.BlockSpec((1, tm, width), lambda i, j: (i, j, 0))
    out_bf = lambda width: jax.ShapeDtypeStruct((b, s, width), BF16)
    return pl.pallas_call(
        functools.partial(_even_in_kernel, tm=tm, wi_scale=N_HEADS ** -0.5 * HEAD ** -0.5),
        grid=(b, s // tm),
        in_specs=[row(d), _const_spec(g.shape), _const_spec(w.shape), _const_spec(wx.shape), _const_spec(cw.shape),
                  _const_spec(gq.shape), _const_spec(gk.shape), _const_spec(lng.shape), _const_spec(lnb.shape),
                  row(LANES), row(LANES), row(LANES), row(LANES)],
        out_specs=[row(CONV_W), row(ATT_W), row(ATT_W), row(ATT_W), row(ATT_W), row(LANES), row(LANES)],
        out_shape=[out_bf(CONV_W), out_bf(ATT_W), out_bf(ATT_W), out_bf(ATT_W), out_bf(ATT_W), out_bf(LANES),
                   jax.ShapeDtypeStruct((b, s, LANES), F32)],
        scratch_shapes=[pltpu.VMEM((tm + 8, CONV_W), F32)],
        compiler_params=_params(("parallel", "arbitrary")),
        name="even_in",
    )(x, g, w, wx, cw, gq, gk, lng, lnb, ca, sa, ci, si)


def _stack_heads(pair, lo):
    zero = jnp.zeros_like(pair)
    return jnp.concatenate([jnp.where(lo, pair, zero), jnp.where(lo, zero, pair)], axis=0)


def _dsa_body(q_ref, k_ref, v_ref, qi_ref, kx_ref, wi_ref, shift_ref, o_ref, key_ref, hi_ref, lo_ref, *,
              tq, tk, topk):
    i = pl.program_id(1)
    lane = _lane_iota()
    lo = lane < HEAD
    hi = jnp.logical_not(lo)
    kf = float(topk)

    kx = kx_ref[0, 0:tk, :]
    wi = wi_ref[0]
    score = None
    for p in range(N_HEADS // 2):
        t = jnp.maximum(_dot_nt(_stack_heads(qi_ref[0, :, p * LANES:(p + 1) * LANES], lo), kx), 0.0)
        for h, th in ((2 * p, t[0:tq]), (2 * p + 1, t[tq:2 * tq])):
            th = th * wi[:, h:h + 1]
            score = th if score is None else score + th

    qpos = i * tq + lax.broadcasted_iota(I32, (tq, 1), 0)
    kpos = lax.broadcasted_iota(I32, (1, tk), 1)
    causal = kpos <= qpos
    score = jnp.where(score == 0.0, 0.0, score)
    bits = pltpu.bitcast(score, I32)
    key = jnp.where(bits < 0, bits ^ 0x7FFFFFFF, bits)
    key = jnp.where(causal, key, KEY_NEG_INF)
    key_ref[:, 0:tk] = key
    hi_ref[:, 0:tk] = lax.shift_right_arithmetic(key, 16).astype(I16)
    lo_ref[:, 0:tk] = ((key & 0xFFFF) - 32768).astype(I16)
    blocks = [slice(cb * LANES, (cb + 1) * LANES) for cb in range(tk // LANES)]

    def count16(ref, t, strict=False):
        t16 = t.astype(I16)
        acc = None
        for sl in blocks:
            m = ref[:, sl] > t16 if strict else ref[:, sl] >= t16
            one = jnp.where(m, jnp.int16(1), jnp.int16(0))
            acc = one if acc is None else acc + one
        return jnp.sum(acc.astype(F32), axis=-1, keepdims=True)

    def search16(ref, need):
        def bit_step(it, t):
            cand = t + lax.shift_left(jnp.int32(1), 15 - it)
            return jnp.where(count16(ref, cand) >= need, cand, t)
        return lax.fori_loop(0, 16, bit_step, jnp.full((tq, 1), -32768, I32))

    t_hi = search16(hi_ref, kf)
    above = count16(hi_ref, t_hi, strict=True)
    t_hi16 = t_hi.astype(I16)
    for sl in blocks:
        lo_ref[:, sl] = jnp.where(hi_ref[:, sl] == t_hi16, lo_ref[:, sl], jnp.int16(-32768))
    t_lo = search16(lo_ref, kf - above)
    thr = lax.shift_left(t_hi, 16) | (t_lo + 32768)

    excess = jnp.logical_and(above + count16(lo_ref, t_lo) > kf, thr > KEY_NEG_INF)
    any_excess = jnp.max(jnp.where(excess, 1.0, 0.0)) > 0.0

    @pl.when(any_excess)
    def _():
        need = kf - jnp.sum(jnp.where(key_ref[:, 0:tk] > thr, 1.0, 0.0), axis=-1, keepdims=True)
        r = lax.broadcasted_iota(I32, (LANES, LANES), 0)
        c = lax.broadcasted_iota(I32, (LANES, LANES), 1)
        upper = jnp.where(r <= c, 1.0, 0.0).astype(BF16)
        seen = jnp.zeros((tq, 1), F32)
        for cb in range(tk // LANES):
            kblk = key_ref[:, cb * LANES:(cb + 1) * LANES]
            tie = kblk == thr
            rank = _dot(jnp.where(tie, 1.0, 0.0).astype(BF16), upper) + seen
            drop = jnp.logical_and(tie, rank > need)
            key_ref[:, cb * LANES:(cb + 1) * LANES] = jnp.where(drop, INT_MIN, kblk)
            seen = rank[:, LANES - 1:LANES]

    sel = key_ref[:, 0:tk] >= jnp.maximum(thr, KEY_NEG_INF + 1)
    selb = jnp.where(sel, 1.0, 0.0).astype(BF16)

    small = shift_ref[0, 0] <= EXP_SAFE_LOGIT

    @pl.when(small)
    def _():
        selb2 = jnp.concatenate([selb, selb], axis=0)
        ones_v = jnp.ones((tk, LANES), BF16)
        outs = []
        for p in range(N_HEADS // 2):
            ps = slice(p * LANES, (p + 1) * LANES)
            e = jnp.exp(_dot_nt(_stack_heads(q_ref[0, :, ps], lo), k_ref[0, 0:tk, ps])).astype(BF16) * selb2
            ov = _dot(e, jnp.concatenate([v_ref[0, 0:tk, ps], ones_v], axis=-1))
            even = ov[0:tq, 0:LANES] / ov[0:tq, LANES:LANES + 1]
            odd = ov[tq:2 * tq, 0:LANES] / ov[tq:2 * tq, LANES:LANES + 1]
            outs.append(jnp.where(lo, even, odd))
        o_ref[0] = jnp.concatenate(outs, axis=-1).astype(BF16)

    @pl.when(jnp.logical_not(small))
    def _():
        bias = jnp.where(sel, 0.0, NEG_BIG)
        outs = []
        for p in range(N_HEADS // 2):
            qp = q_ref[0, :, p * LANES:(p + 1) * LANES]
            kp = k_ref[0, 0:tk, p * LANES:(p + 1) * LANES]
            vp = v_ref[0, 0:tk, p * LANES:(p + 1) * LANES]
            halves = []
            for m, other in ((lo, HEAD), (hi, 0)):
                lg = _dot_nt(jnp.where(m, qp, jnp.zeros_like(qp)), kp) + bias
                e = jnp.exp(lg - jnp.max(lg, axis=-1, keepdims=True)).astype(BF16)
                ov = _dot(e, jnp.where(m, vp, jnp.ones_like(vp)))
                halves.append(ov / ov[:, other:other + 1])
            outs.append(jnp.where(lo, halves[0], halves[1]))
        o_ref[0] = jnp.concatenate(outs, axis=-1).astype(BF16)


def _dsa_kernel(*refs, tq, var_w, n_var, topk):
    i = pl.program_id(1)
    per = var_w // tq
    for j in range(n_var):
        @pl.when(jnp.logical_and(i >= j * per, i < (j + 1) * per))
        def _():
            _dsa_body(*refs, tq=tq, tk=(j + 1) * var_w, topk=topk)


def _dsa(q, k, v, qi, kx, wi, shift, *, tq):
    b, s, _ = q.shape
    topk = min(TOPK_MAX, s // 4)
    var_w = min(DSA_KEY_STEP, s)
    qrow = lambda width: pl.BlockSpec((1, tq, width), lambda i, j: (i, j, 0))
    full = lambda width: pl.BlockSpec((1, s, width), lambda i, j: (i, 0, 0))
    return pl.pallas_call(
        functools.partial(_dsa_kernel, tq=tq, var_w=var_w, n_var=s // var_w, topk=topk),
        grid=(b, s // tq),
        in_specs=[qrow(ATT_W), full(ATT_W), full(ATT_W), qrow(ATT_W), full(kx.shape[2]), qrow(LANES),
                  pl.BlockSpec(memory_space=pltpu.SMEM)],
        out_specs=qrow(ATT_W),
        out_shape=jax.ShapeDtypeStruct((b, s, ATT_W), BF16),
        scratch_shapes=[pltpu.VMEM((tq, s), I32), pltpu.VMEM((tq, s), I16), pltpu.VMEM((tq, s), I16)],
        compiler_params=_params(("parallel", "arbitrary")),
        name="dsa",
    )(q, k, v, qi, kx, wi, shift)


def _post_kernel(*refs, n_y, ff_chunk):
    h_ref = refs[0]
    y_refs = refs[1:1 + n_y]
    (wo_ref, p_ref, gf_ref, wg_ref, wu_ref, wd_ref, wp_ref, gpost_ref, gple_ref, wgate_ref,
     o_ref, acc_ref) = refs[1 + n_y:]

    y = y_refs[0][...] if n_y == 1 else jnp.concatenate([r[...] for r in y_refs], axis=-1)
    h1 = h_ref[...] + _dot(y, wo_ref[...])
    f = _rms(h1, gf_ref[...]).astype(BF16)
    acc_ref[...] = h1
    d_ff = wg_ref.shape[1]
    for c in range(d_ff // ff_chunk):
        cs = slice(c * ff_chunk, (c + 1) * ff_chunk)
        gate = _dot(f, wg_ref[:, cs])
        act = (gate * jax.nn.sigmoid(gate) * _dot(f, wu_ref[:, cs])).astype(BF16)
        acc_ref[...] += _dot(act, wd_ref[cs, :])
    h2 = acc_ref[...]
    e = _rms(_dot(p_ref[...].astype(BF16), wp_ref[...]), gpost_ref[...])
    gate = jax.nn.sigmoid(_dot(_rms(h2, gple_ref[...]).astype(BF16), wgate_ref[...]))
    o_ref[...] = h2 + e * gate


def _post(h, ys, wo, p, gf, wg, wu, wd, wp, gpost, gple, wgate, *, tm, ff_chunk=256):
    r, d = h.shape
    row = lambda width: pl.BlockSpec((tm, width), lambda i: (i, 0))
    consts = [gf, wg, wu, wd, wp, gpost, gple, wgate]
    return pl.pallas_call(
        functools.partial(_post_kernel, n_y=len(ys), ff_chunk=ff_chunk),
        grid=(r // tm,),
        in_specs=([row(d)] + [row(y.shape[1]) for y in ys] + [_const_spec(wo.shape), row(p.shape[1])]
                  + [_const_spec(c.shape) for c in consts]),
        out_specs=row(d),
        out_shape=jax.ShapeDtypeStruct((r, d), F32),
        scratch_shapes=[pltpu.VMEM((tm, d), F32)],
        compiler_params=_params(("parallel",)),
        name="post",
    )(h, *ys, wo, p, *consts)


def _softplus(x):
    return jnp.maximum(x, 0.0) + jnp.log(1.0 + jnp.exp(-jnp.abs(x)))


def _odd_in_kernel(x_ref, g_ref, wqkv_ref, wz_ref, wab_ref, wabt_ref, cw_ref, acol_ref, dtcol_ref,
                   arow_ref, dtrow_ref,
                   q_ref, k_ref, v_ref, z_ref, gb_ref, gt_ref, ubuf_ref, *, tm):
    s = pl.program_id(1)
    hn = _rms(x_ref[0], g_ref[...]).astype(BF16)
    n_qk = 2 * DN_HEADS
    width = wqkv_ref.shape[1]

    @pl.when(s == 0)
    def _():
        ubuf_ref[0:8, :] = jnp.zeros((8, width), F32)

    @pl.when(s > 0)
    def _():
        ubuf_ref[0:8, :] = ubuf_ref[tm:tm + 8, :]

    ubuf_ref[8:tm + 8, :] = _dot(hn, wqkv_ref[...])
    for cb in range(width // LANES):
        cs = slice(cb * LANES, (cb + 1) * LANES)
        y = (cw_ref[0:1, cs] * ubuf_ref[5:tm + 5, cs] + cw_ref[1:2, cs] * ubuf_ref[6:tm + 6, cs]
             + cw_ref[2:3, cs] * ubuf_ref[7:tm + 7, cs] + cw_ref[3:4, cs] * ubuf_ref[8:tm + 8, cs])
        y = y * jax.nn.sigmoid(y)
        if cb < n_qk:
            y = y * lax.rsqrt(jnp.sum(y * y, axis=-1, keepdims=True) + EPS)
            if cb < DN_HEADS:
                q_ref[0, :, cs] = y * (DN_D ** -0.5)
            else:
                k_ref[0, :, slice((cb - DN_HEADS) * LANES, (cb - DN_HEADS + 1) * LANES)] = y
        else:
            v_ref[0, :, slice((cb - n_qk) * LANES, (cb - n_qk + 1) * LANES)] = y

    z_ref[0] = _dot(hn, wz_ref[...])

    ab = _dot(hn, wab_ref[...])
    lane = _lane_iota()
    g_tok = -jnp.exp(arow_ref[...]) * _softplus(ab + dtrow_ref[...])
    gb_ref[0] = jnp.where(lane < DN_HEADS, g_tok, jax.nn.sigmoid(ab))
    at = _dot_nt(wabt_ref[...], hn)
    gt_ref[0] = -jnp.exp(acol_ref[...]) * _softplus(at + dtcol_ref[...])


def _odd_in(x, g, wqkv, wz, wab, wabt, cw, acol, dtcol, arow, dtrow, *, tm):
    b, s, d = x.shape
    hd = DN_HEADS * DN_D
    row = lambda width: pl.BlockSpec((1, tm, width), lambda i, j: (i, j, 0))
    consts = [g, wqkv, wz, wab, wabt, cw, acol, dtcol, arow, dtrow]
    f32 = lambda *shape: jax.ShapeDtypeStruct(shape, F32)
    return pl.pallas_call(
        functools.partial(_odd_in_kernel, tm=tm),
        grid=(b, s // tm),
        in_specs=[row(d)] + [_const_spec(c.shape) for c in consts],
        out_specs=[row(hd), row(hd), row(hd), row(hd), row(LANES),
                   pl.BlockSpec((1, DN_HEADS, tm), lambda i, j: (i, 0, j))],
        out_shape=[f32(b, s, hd), f32(b, s, hd), f32(b, s, hd), f32(b, s, hd), f32(b, s, LANES),
                   f32(b, DN_HEADS, s)],
        scratch_shapes=[pltpu.VMEM((tm + 8, wqkv.shape[1]), F32)],
        compiler_params=_params(("parallel", "arbitrary")),
        name="odd_in",
    )(x, *consts)


def _delta_kernel(q_ref, k_ref, v_ref, z_ref, gb_ref, gt_ref, on_ref, y_ref, state_ref, *, n_chunk):
    c = DN_CHUNK

    @pl.when(pl.program_id(1) == 0)
    def _():
        state_ref[...] = jnp.zeros(state_ref.shape, F32)

    ri = lax.broadcasted_iota(I32, (c, c), 0)
    cj = lax.broadcasted_iota(I32, (c, c), 1)
    tril = ri >= cj
    strict = ri > cj
    eye = jnp.where(ri == cj, 1.0, 0.0)
    lower = jnp.where(tril, 1.0, 0.0).astype(BF16)
    upper = jnp.where(ri <= cj, 1.0, 0.0).astype(BF16)

    probs = [(n, h) for n in range(n_chunk) for h in range(DN_HEADS)]
    rows = lambda n: slice(n * c, (n + 1) * c)
    cols = lambda h: slice(h * DN_D, (h + 1) * DN_D)

    gbs, gc_col, gc_row, e_col, d_col, e_last = [], [], [], [], [], []
    for n in range(n_chunk):
        gb = gb_ref[0, rows(n), :]
        g1, g2, g3 = _split3(gb)
        gcc = _dot(lower, g1) + (_dot(lower, g2) + _dot(lower, g3))
        t1, t2, t3 = _split3(gt_ref[0, n])
        gcr = _dot(t1, upper) + (_dot(t2, upper) + _dot(t3, upper))
        g_last = gcc[c - 1:c, :]
        gbs.append(gb)
        gc_col.append(gcc)
        gc_row.append(gcr)
        e_col.append(jnp.exp(gcc))
        d_col.append(jnp.exp(g_last - gcc))
        e_last.append(jnp.exp(g_last))

    k16, kb, decay = {}, {}, {}
    for n, h in probs:
        kh = k_ref[0, rows(n), cols(h)]
        k16[n, h] = kh.astype(BF16)
        kb[n, h] = kh * gbs[n][:, DN_HEADS + h:DN_HEADS + h + 1]
        diff = gc_col[n][:, h:h + 1] - gc_row[n][h:h + 1, :]
        decay[n, h] = jnp.exp(jnp.where(tril, diff, NEG_BIG))
    a = {p: jnp.where(strict, _dot_nt(kb[p].astype(BF16), k16[p]) * decay[p], 0.0) for p in probs}
    qk16 = {}
    for n, h in probs:
        qh = q_ref[0, rows(n), cols(h)].astype(BF16)
        qk16[n, h] = jnp.where(tril, _dot_nt(qh, k16[n, h]) * decay[n, h], 0.0).astype(BF16)

    pw = {p: -a[p] for p in probs}
    tinv = {p: eye + pw[p] for p in probs}
    for _ in range(5):
        p16 = {p: pw[p].astype(BF16) for p in probs}
        pw = {p: _dot(p16[p], p16[p]) for p in probs}
        tinv = {p: tinv[p] + _dot(tinv[p].astype(BF16), pw[p].astype(BF16)) for p in probs}
    t16 = {p: tinv[p].astype(BF16) for p in probs}
    u, w16 = {}, {}
    for n, h in probs:
        beta = gbs[n][:, DN_HEADS + h:DN_HEADS + h + 1]
        u[n, h] = _dot(t16[n, h], (v_ref[0, rows(n), cols(h)] * beta).astype(BF16))
        w16[n, h] = _dot(t16[n, h], (kb[n, h] * e_col[n][:, h:h + 1]).astype(BF16)).astype(BF16)

    heads = range(DN_HEADS)
    for n in range(n_chunk):
        st = [state_ref[h] for h in heads]
        s16 = [x.astype(BF16) for x in st]
        v16 = [(u[n, h] - _dot(w16[n, h], s16[h])).astype(BF16) for h in heads]
        for h in heads:
            qg = (q_ref[0, rows(n), cols(h)] * e_col[n][:, h:h + 1]).astype(BF16)
            o = _dot(qg, s16[h]) + _dot(qk16[n, h], v16[h])
            k_dec = (k_ref[0, rows(n), cols(h)] * d_col[n][:, h:h + 1]).astype(BF16)
            state_ref[h] = st[h] * e_last[n][:, h:h + 1] + _dot_tn(k_dec, v16[h])
            zh = z_ref[0, rows(n), cols(h)]
            on = o * lax.rsqrt(jnp.mean(o * o, axis=-1, keepdims=True) + EPS) * on_ref[...]
            y_ref[0, rows(n), cols(h)] = (on * (zh * jax.nn.sigmoid(zh))).astype(BF16)


def _delta(q, k, v, z, gb, gt, on, *, tc):
    b, s, hd = q.shape
    n_chunk = tc // DN_CHUNK
    row = lambda width: pl.BlockSpec((1, tc, width), lambda i, j: (i, j, 0))
    return pl.pallas_call(
        functools.partial(_delta_kernel, n_chunk=n_chunk),
        grid=(b, s // tc),
        in_specs=[row(hd), row(hd), row(hd), row(hd), row(LANES),
                  pl.BlockSpec((1, n_chunk, DN_HEADS, DN_CHUNK), lambda i, j: (i, j, 0, 0)),
                  _const_spec(on.shape)],
        out_specs=row(hd),
        out_shape=jax.ShapeDtypeStruct((b, s, hd), BF16),
        scratch_shapes=[pltpu.VMEM((DN_HEADS, DN_D, DN_D), F32)],
        compiler_params=_params(("parallel", "arbitrary")),
        name="delta",
    )(q, k, v, z, gb, gt, on)


def _rope_tables(positions, dim):
    inv_freq = 1.0 / (ROPE_THETA ** (jnp.arange(0, dim, 2, dtype=F32) / dim))
    ang = positions.astype(F32)[..., None] * inv_freq
    return jnp.cos(ang), jnp.sin(ang)


def _row(v):
    return v.reshape(1, -1).astype(F32)


def kernel(x, p, positions, norm_mix_g, norm_ffn_g, ev_w_in, ev_conv_w, ev_q_norm_g, ev_k_norm_g, ev_ik_ln_g, ev_ik_ln_b, ev_w_out, od_w_in, od_conv_w, od_a_log, od_dt_bias, od_o_norm_g, od_w_out, ffn_w_gate, ffn_w_up, ffn_w_down, ple_w_proj, ple_post_norm_g, ple_norm_g, ple_w_gate):
    b, s, d = x.shape
    depth = p.shape[0]
    tm = min(512, s)

    cos_a, sin_a = _rope_tables(positions, HEAD)
    cos_i, sin_i = _rope_tables(positions, HEAD // 2)
    ca = jnp.tile(jnp.concatenate([cos_a, cos_a], -1), (1, 1, 2))
    sa = jnp.tile(jnp.concatenate([-sin_a, sin_a], -1), (1, 1, 2))
    ones = jnp.ones(cos_i.shape[:-1] + (HEAD // 2,), F32)
    ci = jnp.tile(jnp.concatenate([cos_i, cos_i, ones], -1), (1, 1, 2))
    si = jnp.tile(jnp.concatenate([-sin_i, sin_i, 0.0 * ones], -1), (1, 1, 2))

    h = x
    for i in range(depth):
        j = i // 2
        if i % 2 == 0:
            w = ev_w_in[j]
            n_main = 3 * CONV_W + 3 * ATT_W
            w_pad = jnp.zeros((d, LANES - HEAD - N_HEADS), w.dtype)
            wx = jnp.concatenate([w[:, n_main:], w_pad], axis=1).astype(BF16)
            pad64 = lambda v: jnp.concatenate([v, jnp.zeros((HEAD,), v.dtype)]).reshape(1, LANES).astype(F32)
            ya, q, k, v, qi, kx, wi = _even_in(
                h, _row(norm_mix_g[i]), w[:, :n_main].astype(BF16), wx, ev_conv_w[j].astype(F32),
                _row(jnp.tile(ev_q_norm_g[j], N_HEADS)), _row(jnp.tile(ev_k_norm_g[j], N_HEADS)),
                pad64(ev_ik_ln_g[j]), pad64(ev_ik_ln_b[j]), ca, sa, ci, si, tm=tm)
            shift = (HEAD ** 0.5 * jnp.max(jnp.abs(ev_q_norm_g[j])) * jnp.max(jnp.abs(ev_k_norm_g[j]))).astype(F32)
            yb = _dsa(q, k, v, qi, kx, wi, shift.reshape(1, 1), tq=min(DSA_TQ, s))
            wo = ev_w_out[j].astype(BF16)
            ys = [ya.reshape(b * s, CONV_W), yb.reshape(b * s, ATT_W)]
        else:
            w = od_w_in[j]
            n_qkv = 3 * DN_HEADS * DN_D
            n_z = DN_HEADS * DN_D
            w_ab = w[:, n_qkv + n_z:]
            wab = jnp.concatenate([w_ab, jnp.zeros((d, LANES - 2 * DN_HEADS), w.dtype)], axis=1).astype(BF16)
            wabt = w_ab[:, :DN_HEADS].T.astype(BF16)
            a_log = od_a_log[j].astype(F32)
            dt_b = od_dt_bias[j].astype(F32)
            pad_row = lambda v: jnp.concatenate([v, jnp.zeros((LANES - DN_HEADS,), F32)]).reshape(1, LANES)
            q, k, v, z, gb, gt = _odd_in(
                h, _row(norm_mix_g[i]), w[:, :n_qkv].astype(BF16), w[:, n_qkv:n_qkv + n_z].astype(BF16),
                wab, wabt, od_conv_w[j].astype(F32), a_log.reshape(DN_HEADS, 1), dt_b.reshape(DN_HEADS, 1),
                pad_row(a_log), pad_row(dt_b), tm=tm)
            gt = gt.reshape(b, DN_HEADS, s // DN_CHUNK, DN_CHUNK).transpose(0, 2, 1, 3)
            y = _delta(q, k, v, z, gb, gt, _row(od_o_norm_g[j]), tc=min(DELTA_TILE, s))
            ys = [y.reshape(b * s, DN_HEADS * DN_D)]
            wo = od_w_out[j].astype(BF16)
        h = _post(h.reshape(b * s, d), ys, wo, p[i].reshape(b * s, -1), _row(norm_ffn_g[i]),
                  ffn_w_gate[i].astype(BF16), ffn_w_up[i].astype(BF16), ffn_w_down[i].astype(BF16),
                  ple_w_proj[i].astype(BF16), _row(ple_post_norm_g[i]), _row(ple_norm_g[i]),
                  ple_w_gate[i].astype(BF16), tm=tm).reshape(b, s, d)
    return h
```

```python
import functools

import jax
import jax.numpy as jnp
from jax import lax
from jax.experimental import pallas as pl
from jax.experimental.pallas import tpu as pltpu

F32 = jnp.float32
BF16 = jnp.bfloat16
I32 = jnp.int32
I16 = jnp.int16

EPS = 1e-6
ROPE_THETA = 10000.0
LANES = 128
HEAD = 64
N_HEADS = 8
CONV_W = 512
ATT_W = 512
TOPK_MAX = 256
DSA_TQ = 128
DSA_KEY_STEP = 512
DN_HEADS = 8
DN_D = 128
DN_CHUNK = 64
DELTA_TILE = 256
VMEM_LIMIT = 56 * 1024 * 1024

KEY_NEG_INF = -2139095041
INT_MIN = -2147483648
NEG_BIG = -1e30
EXP_SAFE_LOGIT = 60.0

NT_DIMS = (((1,), (1,)), ((), ()))
TN_DIMS = (((0,), (0,)), ((), ()))


def _dot(a, b):
    return jnp.dot(a, b, preferred_element_type=F32)


def _dot_nt(a, b):
    return lax.dot_general(a, b, NT_DIMS, preferred_element_type=F32)


def _dot_tn(a, b):
    return lax.dot_general(a, b, TN_DIMS, preferred_element_type=F32)


def _rms(x, g):
    ms = jnp.mean(x * x, axis=-1, keepdims=True)
    return x * lax.rsqrt(ms + EPS) * g


def _split3(x):
    x1 = x.astype(BF16)
    r = x - x1.astype(F32)
    x2 = r.astype(BF16)
    x3 = (r - x2.astype(F32)).astype(BF16)
    return x1, x2, x3


def _const_spec(shape):
    nd = len(shape)
    return pl.BlockSpec(shape, lambda *_: (0,) * nd, pipeline_mode=pl.Buffered(1))


def _params(sem):
    return pltpu.CompilerParams(dimension_semantics=sem, vmem_limit_bytes=VMEM_LIMIT)


def _lane_iota():
    return lax.broadcasted_iota(I32, (1, LANES), 1)


def _swap_halves(x, half):
    lane = _lane_iota()
    first = (lane % (2 * half)) < half
    return jnp.where(first, pltpu.roll(x, LANES - half, axis=1), pltpu.roll(x, half, axis=1))


def _even_in_kernel(x_ref, g_ref, w_ref, wx_ref, cw_ref, gq_ref, gk_ref, lng_ref, lnb_ref,
                    ca_ref, sa_ref, ci_ref, si_ref,
                    ya_ref, q_ref, k_ref, v_ref, qi_ref, kx_ref, wi_ref, ubuf_ref, *, tm, wi_scale):
    s = pl.program_id(1)
    hn = _rms(x_ref[0], g_ref[...]).astype(BF16)
    lane = _lane_iota()
    lo = lane < HEAD

    u = _dot(hn, w_ref[:, 512:1024]) * _dot(hn, w_ref[:, 1024:1536])

    @pl.when(s == 0)
    def _():
        ubuf_ref[0:8, :] = jnp.zeros((8, CONV_W), F32)

    @pl.when(s > 0)
    def _():
        ubuf_ref[0:8, :] = ubuf_ref[tm:tm + 8, :]

    ubuf_ref[8:tm + 8, :] = u
    conv = (cw_ref[0:1, :] * ubuf_ref[6:tm + 6, :] + cw_ref[1:2, :] * ubuf_ref[7:tm + 7, :]
            + cw_ref[2:3, :] * u)
    ya_ref[0] = (_dot(hn, w_ref[:, 0:512]) * conv).astype(BF16)

    ca = ca_ref[0]
    sa = sa_ref[0]
    ci = ci_ref[0]
    si = si_ref[0]

    def head_norm_rope(z, gain, scale):
        outs = []
        for j in range(ATT_W // LANES):
            xb = z[:, j * LANES:(j + 1) * LANES]
            sq = xb * xb
            s_lo = jnp.sum(jnp.where(lo, sq, 0.0), axis=-1, keepdims=True)
            s_hi = jnp.sum(jnp.where(lo, 0.0, sq), axis=-1, keepdims=True)
            r = jnp.where(lo, lax.rsqrt(s_lo * (1.0 / HEAD) + EPS), lax.rsqrt(s_hi * (1.0 / HEAD) + EPS))
            xn = xb * r * gain[:, j * LANES:(j + 1) * LANES]
            outs.append((xn * ca + _swap_halves(xn, HEAD // 2) * sa) * scale)
        return jnp.concatenate(outs, axis=-1)

    q_ref[0] = head_norm_rope(_dot(hn, w_ref[:, 1536:2048]), gq_ref[...], HEAD ** -0.5).astype(BF16)
    k_ref[0] = head_norm_rope(_dot(hn, w_ref[:, 2048:2560]), gk_ref[...], 1.0).astype(BF16)
    v_ref[0] = _dot(hn, w_ref[:, 2560:3072]).astype(BF16)

    zidx = _dot(hn, wx_ref[...])
    zqi = zidx[:, 0:ATT_W]
    outs = []
    for j in range(ATT_W // LANES):
        xb = zqi[:, j * LANES:(j + 1) * LANES]
        outs.append(xb * ci + _swap_halves(xb, HEAD // 4) * si)
    qi_ref[0] = jnp.concatenate(outs, axis=-1).astype(BF16)

    zkw = zidx[:, ATT_W:ATT_W + LANES]
    mu = jnp.sum(jnp.where(lo, zkw, 0.0), axis=-1, keepdims=True) * (1.0 / HEAD)
    d = zkw - mu
    var = jnp.sum(jnp.where(lo, d * d, 0.0), axis=-1, keepdims=True) * (1.0 / HEAD)
    kn = d * lax.rsqrt(var + EPS) * lng_ref[...] + lnb_ref[...]
    kr = jnp.where(lo, kn * ci + _swap_halves(kn, HEAD // 4) * si, 0.0)
    kx_ref[0] = (kr + pltpu.roll(kr, HEAD, axis=1)).astype(BF16)
    wi_ref[0] = jnp.where(lane < N_HEADS, pltpu.roll(zkw, HEAD, axis=1) * wi_scale, 0.0)


def _even_in(x, g, w, wx, cw, gq, gk, lng, lnb, ca, sa, ci, si, *, tm):
    b, s, d = x.shape
    row = lambda width: pl.BlockSpec((1, tm, width), lambda i, j: (i, j, 0))
    out_bf = lambda width: jax.ShapeDtypeStruct((b, s, width), BF16)
    return pl.pallas_call(
        functools.partial(_even_in_kernel, tm=tm, wi_scale=N_HEADS ** -0.5 * HEAD ** -0.5),
        grid=(b, s // tm),
        in_specs=[row(d), _const_spec(g.shape), _const_spec(w.shape), _const_spec(wx.shape), _const_spec(cw.shape),
                  _const_spec(gq.shape), _const_spec(gk.shape), _const_spec(lng.shape), _const_spec(lnb.shape),
                  row(LANES), row(LANES), row(LANES), row(LANES)],
        out_specs=[row(CONV_W), row(ATT_W), row(ATT_W), row(ATT_W), row(ATT_W), row(LANES), row(LANES)],
        out_shape=[out_bf(CONV_W), out_bf(ATT_W), out_bf(ATT_W), out_bf(ATT_W), out_bf(ATT_W), out_bf(LANES),
                   jax.ShapeDtypeStruct((b, s, LANES), F32)],
        scratch_shapes=[pltpu.VMEM((tm + 8, CONV_W), F32)],
        compiler_params=_params(("parallel", "arbitrary")),
        name="even_in",
    )(x, g, w, wx, cw, gq, gk, lng, lnb, ca, sa, ci, si)


def _stack_heads(pair, lo):
    zero = jnp.zeros_like(pair)
    return jnp.concatenate([jnp.where(lo, pair, zero), jnp.where(lo, zero, pair)], axis=0)


def _dsa_body(q_ref, k_ref, v_ref, qi_ref, kx_ref, wi_ref, shift_ref, o_ref, key_ref, hi_ref, lo_ref, thr_ref, *,
              tq, tk, topk, i_first):
    i = pl.program_id(1)
    lane = _lane_iota()
    lo = lane < HEAD
    hi = jnp.logical_not(lo)
    kf = float(topk)

    kx = kx_ref[0, 0:tk, :]
    wi = wi_ref[0]
    score = None
    for p in range(N_HEADS // 2):
        t = jnp.maximum(_dot_nt(_stack_heads(qi_ref[0, :, p * LANES:(p + 1) * LANES], lo), kx), 0.0)
        for h, th in ((2 * p, t[0:tq]), (2 * p + 1, t[tq:2 * tq])):
            th = th * wi[:, h:h + 1]
            score = th if score is None else score + th

    qpos = i * tq + lax.broadcasted_iota(I32, (tq, 1), 0)
    kpos = lax.broadcasted_iota(I32, (1, tk), 1)
    causal = kpos <= qpos
    score = jnp.where(score == 0.0, 0.0, score)
    bits = pltpu.bitcast(score, I32)
    key = jnp.where(bits < 0, bits ^ 0x7FFFFFFF, bits)
    key = jnp.where(causal, key, KEY_NEG_INF)
    key_ref[:, 0:tk] = key
    hi_ref[:, 0:tk] = lax.shift_right_arithmetic(key, 16).astype(I16)
    lo_ref[:, 0:tk] = ((key & 0xFFFF) - 32768).astype(I16)
    blocks = [slice(cb * LANES, (cb + 1) * LANES) for cb in range(tk // LANES)]

    def count16(ref, t, strict=False):
        t16 = t.astype(I16)
        acc = None
        for sl in blocks:
            m = ref[:, sl] > t16 if strict else ref[:, sl] >= t16
            one = jnp.where(m, jnp.int16(1), jnp.int16(0))
            acc = one if acc is None else acc + one
        return jnp.sum(acc.astype(F32), axis=-1, keepdims=True)

    def search16(ref, need):
        def bit_step(it, t):
            cand = t + lax.shift_left(jnp.int32(1), 15 - it)
            return jnp.where(count16(ref, cand) >= need, cand, t)
        return lax.fori_loop(0, 16, bit_step, jnp.full((tq, 1), -32768, I32))

    def find_threshold():
        t_hi = search16(hi_ref, kf)
        above = count16(hi_ref, t_hi, strict=True)
        t_hi16 = t_hi.astype(I16)
        for sl in blocks:
            lo_ref[:, sl] = jnp.where(hi_ref[:, sl] == t_hi16, lo_ref[:, sl], jnp.int16(-32768))
        t_lo = search16(lo_ref, kf - above)
        thr = lax.shift_left(t_hi, 16) | (t_lo + 32768)

        excess = jnp.logical_and(above + count16(lo_ref, t_lo) > kf, thr > KEY_NEG_INF)
        any_excess = jnp.max(jnp.where(excess, 1.0, 0.0)) > 0.0

        @pl.when(any_excess)
        def _():
            need = kf - jnp.sum(jnp.where(key_ref[:, 0:tk] > thr, 1.0, 0.0), axis=-1, keepdims=True)
            r = lax.broadcasted_iota(I32, (LANES, LANES), 0)
            c = lax.broadcasted_iota(I32, (LANES, LANES), 1)
            upper = jnp.where(r <= c, 1.0, 0.0).astype(BF16)
            seen = jnp.zeros((tq, 1), F32)
            for cb in range(tk // LANES):
                kblk = key_ref[:, cb * LANES:(cb + 1) * LANES]
                tie = kblk == thr
                rank = _dot(jnp.where(tie, 1.0, 0.0).astype(BF16), upper) + seen
                drop = jnp.logical_and(tie, rank > need)
                key_ref[:, cb * LANES:(cb + 1) * LANES] = jnp.where(drop, INT_MIN, kblk)
                seen = rank[:, LANES - 1:LANES]

        return thr

    if (i_first + 1) * tq <= topk:
        fits = (i + 1) * tq <= topk

        @pl.when(fits)
        def _():
            thr_ref[...] = jnp.full(thr_ref.shape, KEY_NEG_INF, I32)

        @pl.when(jnp.logical_not(fits))
        def _():
            thr_ref[...] = jnp.broadcast_to(find_threshold(), thr_ref.shape)

        thr = thr_ref[:, 0:1]
    else:
        thr = find_threshold()

    sel = key_ref[:, 0:tk] >= jnp.maximum(thr, KEY_NEG_INF + 1)
    selb = jnp.where(sel, 1.0, 0.0).astype(BF16)

    small = shift_ref[0, 0] <= EXP_SAFE_LOGIT

    @pl.when(small)
    def _():
        selb2 = jnp.concatenate([selb, selb], axis=0)
        ones_v = jnp.ones((tk, LANES), BF16)
        outs = []
        for p in range(N_HEADS // 2):
            ps = slice(p * LANES, (p + 1) * LANES)
            e = jnp.exp(_dot_nt(_stack_heads(q_ref[0, :, ps], lo), k_ref[0, 0:tk, ps])).astype(BF16) * selb2
            ov = _dot(e, jnp.concatenate([v_ref[0, 0:tk, ps], ones_v], axis=-1))
            even = ov[0:tq, 0:LANES] / ov[0:tq, LANES:LANES + 1]
            odd = ov[tq:2 * tq, 0:LANES] / ov[tq:2 * tq, LANES:LANES + 1]
            outs.append(jnp.where(lo, even, odd))
        o_ref[0] = jnp.concatenate(outs, axis=-1).astype(BF16)

    @pl.when(jnp.logical_not(small))
    def _():
        bias = jnp.where(sel, 0.0, NEG_BIG)
        outs = []
        for p in range(N_HEADS // 2):
            qp = q_ref[0, :, p * LANES:(p + 1) * LANES]
            kp = k_ref[0, 0:tk, p * LANES:(p + 1) * LANES]
            vp = v_ref[0, 0:tk, p * LANES:(p + 1) * LANES]
            halves = []
            for m, other in ((lo, HEAD), (hi, 0)):
                lg = _dot_nt(jnp.where(m, qp, jnp.zeros_like(qp)), kp) + bias
                e = jnp.exp(lg - jnp.max(lg, axis=-1, keepdims=True)).astype(BF16)
                ov = _dot(e, jnp.where(m, vp, jnp.ones_like(vp)))
                halves.append(ov / ov[:, other:other + 1])
            outs.append(jnp.where(lo, halves[0], halves[1]))
        o_ref[0] = jnp.concatenate(outs, axis=-1).astype(BF16)


def _dsa_kernel(*refs, tq, var_w, n_var, topk):
    i = pl.program_id(1)
    per = var_w // tq
    for j in range(n_var):
        @pl.when(jnp.logical_and(i >= j * per, i < (j + 1) * per))
        def _():
            _dsa_body(*refs, tq=tq, tk=(j + 1) * var_w, topk=topk, i_first=j * per)


def _dsa(q, k, v, qi, kx, wi, shift, *, tq):
    b, s, _ = q.shape
    topk = min(TOPK_MAX, s // 4)
    var_w = min(DSA_KEY_STEP, s)
    qrow = lambda width: pl.BlockSpec((1, tq, width), lambda i, j: (i, j, 0))
    full = lambda width: pl.BlockSpec((1, s, width), lambda i, j: (i, 0, 0))
    return pl.pallas_call(
        functools.partial(_dsa_kernel, tq=tq, var_w=var_w, n_var=s // var_w, topk=topk),
        grid=(b, s // tq),
        in_specs=[qrow(ATT_W), full(ATT_W), full(ATT_W), qrow(ATT_W), full(kx.shape[2]), qrow(LANES),
                  pl.BlockSpec(memory_space=pltpu.SMEM)],
        out_specs=qrow(ATT_W),
        out_shape=jax.ShapeDtypeStruct((b, s, ATT_W), BF16),
        scratch_shapes=[pltpu.VMEM((tq, s), I32), pltpu.VMEM((tq, s), I16), pltpu.VMEM((tq, s), I16),
                        pltpu.VMEM((tq, LANES), I32)],
        compiler_params=_params(("parallel", "arbitrary")),
        name="dsa",
    )(q, k, v, qi, kx, wi, shift)


def _post_kernel(*refs, n_y, ff_chunk):
    h_ref = refs[0]
    y_refs = refs[1:1 + n_y]
    (wo_ref, p_ref, gf_ref, wg_ref, wu_ref, wd_ref, wp_ref, gpost_ref, gple_ref, wgate_ref,
     o_ref, acc_ref) = refs[1 + n_y:]

    y = y_refs[0][...] if n_y == 1 else jnp.concatenate([r[...] for r in y_refs], axis=-1)
    h1 = h_ref[...] + _dot(y, wo_ref[...])
    f = _rms(h1, gf_ref[...]).astype(BF16)
    acc_ref[...] = h1
    d_ff = wg_ref.shape[1]
    for c in range(d_ff // ff_chunk):
        cs = slice(c * ff_chunk, (c + 1) * ff_chunk)
        gate = _dot(f, wg_ref[:, cs])
        act = (gate * jax.nn.sigmoid(gate) * _dot(f, wu_ref[:, cs])).astype(BF16)
        acc_ref[...] += _dot(act, wd_ref[cs, :])
    h2 = acc_ref[...]
    e = _rms(_dot(p_ref[...].astype(BF16), wp_ref[...]), gpost_ref[...])
    gate = jax.nn.sigmoid(_dot(_rms(h2, gple_ref[...]).astype(BF16), wgate_ref[...]))
    o_ref[...] = h2 + e * gate


def _post(h, ys, wo, p, gf, wg, wu, wd, wp, gpost, gple, wgate, *, tm, ff_chunk=256):
    r, d = h.shape
    row = lambda width: pl.BlockSpec((tm, width), lambda i: (i, 0))
    consts = [gf, wg, wu, wd, wp, gpost, gple, wgate]
    return pl.pallas_call(
        functools.partial(_post_kernel, n_y=len(ys), ff_chunk=ff_chunk),
        grid=(r // tm,),
        in_specs=([row(d)] + [row(y.shape[1]) for y in ys] + [_const_spec(wo.shape), row(p.shape[1])]
                  + [_const_spec(c.shape) for c in consts]),
        out_specs=row(d),
        out_shape=jax.ShapeDtypeStruct((r, d), F32),
        scratch_shapes=[pltpu.VMEM((tm, d), F32)],
        compiler_params=_params(("parallel",)),
        name="post",
    )(h, *ys, wo, p, *consts)


def _softplus(x):
    return jnp.maximum(x, 0.0) + jnp.log(1.0 + jnp.exp(-jnp.abs(x)))


def _odd_in_kernel(x_ref, g_ref, wqkv_ref, wz_ref, wab_ref, wabt_ref, cw_ref, acol_ref, dtcol_ref,
                   arow_ref, dtrow_ref,
                   q_ref, k_ref, v_ref, z_ref, gb_ref, gt_ref, ubuf_ref, *, tm):
    s = pl.program_id(1)
    hn = _rms(x_ref[0], g_ref[...]).astype(BF16)
    n_qk = 2 * DN_HEADS
    width = wqkv_ref.shape[1]

    @pl.when(s == 0)
    def _():
        ubuf_ref[0:8, :] = jnp.zeros((8, width), F32)

    @pl.when(s > 0)
    def _():
        ubuf_ref[0:8, :] = ubuf_ref[tm:tm + 8, :]

    ubuf_ref[8:tm + 8, :] = _dot(hn, wqkv_ref[...])
    for cb in range(width // LANES):
        cs = slice(cb * LANES, (cb + 1) * LANES)
        y = (cw_ref[0:1, cs] * ubuf_ref[5:tm + 5, cs] + cw_ref[1:2, cs] * ubuf_ref[6:tm + 6, cs]
             + cw_ref[2:3, cs] * ubuf_ref[7:tm + 7, cs] + cw_ref[3:4, cs] * ubuf_ref[8:tm + 8, cs])
        y = y * jax.nn.sigmoid(y)
        if cb < n_qk:
            y = y * lax.rsqrt(jnp.sum(y * y, axis=-1, keepdims=True) + EPS)
            if cb < DN_HEADS:
                q_ref[0, :, cs] = y * (DN_D ** -0.5)
            else:
                k_ref[0, :, slice((cb - DN_HEADS) * LANES, (cb - DN_HEADS + 1) * LANES)] = y
        else:
            v_ref[0, :, slice((cb - n_qk) * LANES, (cb - n_qk + 1) * LANES)] = y

    z_ref[0] = _dot(hn, wz_ref[...])

    ab = _dot(hn, wab_ref[...])
    lane = _lane_iota()
    g_tok = -jnp.exp(arow_ref[...]) * _softplus(ab + dtrow_ref[...])
    gb_ref[0] = jnp.where(lane < DN_HEADS, g_tok, jax.nn.sigmoid(ab))
    at = _dot_nt(wabt_ref[...], hn)
    gt_ref[0] = -jnp.exp(acol_ref[...]) * _softplus(at + dtcol_ref[...])


def _odd_in(x, g, wqkv, wz, wab, wabt, cw, acol, dtcol, arow, dtrow, *, tm):
    b, s, d = x.shape
    hd = DN_HEADS * DN_D
    row = lambda width: pl.BlockSpec((1, tm, width), lambda i, j: (i, j, 0))
    consts = [g, wqkv, wz, wab, wabt, cw, acol, dtcol, arow, dtrow]
    f32 = lambda *shape: jax.ShapeDtypeStruct(shape, F32)
    return pl.pallas_call(
        functools.partial(_odd_in_kernel, tm=tm),
        grid=(b, s // tm),
        in_specs=[row(d)] + [_const_spec(c.shape) for c in consts],
        out_specs=[row(hd), row(hd), row(hd), row(hd), row(LANES),
                   pl.BlockSpec((1, DN_HEADS, tm), lambda i, j: (i, 0, j))],
        out_shape=[f32(b, s, hd), f32(b, s, hd), f32(b, s, hd), f32(b, s, hd), f32(b, s, LANES),
                   f32(b, DN_HEADS, s)],
        scratch_shapes=[pltpu.VMEM((tm + 8, wqkv.shape[1]), F32)],
        compiler_params=_params(("parallel", "arbitrary")),
        name="odd_in",
    )(x, *consts)


def _delta_kernel(q_ref, k_ref, v_ref, z_ref, gb_ref, gt_ref, on_ref, y_ref, state_ref, *, n_chunk):
    c = DN_CHUNK

    @pl.when(pl.program_id(1) == 0)
    def _():
        state_ref[...] = jnp.zeros(state_ref.shape, F32)

    ri = lax.broadcasted_iota(I32, (c, c), 0)
    cj = lax.broadcasted_iota(I32, (c, c), 1)
    tril = ri >= cj
    strict = ri > cj
    eye = jnp.where(ri == cj, 1.0, 0.0)
    lower = jnp.where(tril, 1.0, 0.0).astype(BF16)
    upper = jnp.where(ri <= cj, 1.0, 0.0).astype(BF16)

    probs = [(n, h) for n in range(n_chunk) for h in range(DN_HEADS)]
    rows = lambda n: slice(n * c, (n + 1) * c)
    cols = lambda h: slice(h * DN_D, (h + 1) * DN_D)

    gbs, gc_col, gc_row, e_col, d_col, e_last = [], [], [], [], [], []
    for n in range(n_chunk):
        gb = gb_ref[0, rows(n), :]
        g1, g2, g3 = _split3(gb)
        gcc = _dot(lower, g1) + (_dot(lower, g2) + _dot(lower, g3))
        t1, t2, t3 = _split3(gt_ref[0, n])
        gcr = _dot(t1, upper) + (_dot(t2, upper) + _dot(t3, upper))
        g_last = gcc[c - 1:c, :]
        gbs.append(gb)
        gc_col.append(gcc)
        gc_row.append(gcr)
        e_col.append(jnp.exp(gcc))
        d_col.append(jnp.exp(g_last - gcc))
        e_last.append(jnp.exp(g_last))

    k16, kb, decay = {}, {}, {}
    for n, h in probs:
        kh = k_ref[0, rows(n), cols(h)]
        k16[n, h] = kh.astype(BF16)
        kb[n, h] = kh * gbs[n][:, DN_HEADS + h:DN_HEADS + h + 1]
        diff = gc_col[n][:, h:h + 1] - gc_row[n][h:h + 1, :]
        decay[n, h] = jnp.exp(jnp.where(tril, diff, NEG_BIG))
    a = {p: jnp.where(strict, _dot_nt(kb[p].astype(BF16), k16[p]) * decay[p], 0.0) for p in probs}
    qk16 = {}
    for n, h in probs:
        qh = q_ref[0, rows(n), cols(h)].astype(BF16)
        qk16[n, h] = jnp.where(tril, _dot_nt(qh, k16[n, h]) * decay[n, h], 0.0).astype(BF16)

    pw = {p: -a[p] for p in probs}
    tinv = {p: eye + pw[p] for p in probs}
    for _ in range(5):
        p16 = {p: pw[p].astype(BF16) for p in probs}
        pw = {p: _dot(p16[p], p16[p]) for p in probs}
        tinv = {p: tinv[p] + _dot(tinv[p].astype(BF16), pw[p].astype(BF16)) for p in probs}
    t16 = {p: tinv[p].astype(BF16) for p in probs}
    u, w16 = {}, {}
    for n, h in probs:
        beta = gbs[n][:, DN_HEADS + h:DN_HEADS + h + 1]
        u[n, h] = _dot(t16[n, h], (v_ref[0, rows(n), cols(h)] * beta).astype(BF16))
        w16[n, h] = _dot(t16[n, h], (kb[n, h] * e_col[n][:, h:h + 1]).astype(BF16)).astype(BF16)

    heads = range(DN_HEADS)
    for n in range(n_chunk):
        st = [state_ref[h] for h in heads]
        s16 = [x.astype(BF16) for x in st]
        v16 = [(u[n, h] - _dot(w16[n, h], s16[h])).astype(BF16) for h in heads]
        for h in heads:
            qg = (q_ref[0, rows(n), cols(h)] * e_col[n][:, h:h + 1]).astype(BF16)
            o = _dot(qg, s16[h]) + _dot(qk16[n, h], v16[h])
            k_dec = (k_ref[0, rows(n), cols(h)] * d_col[n][:, h:h + 1]).astype(BF16)
            state_ref[h] = st[h] * e_last[n][:, h:h + 1] + _dot_tn(k_dec, v16[h])
            zh = z_ref[0, rows(n), cols(h)]
            on = o * lax.rsqrt(jnp.mean(o * o, axis=-1, keepdims=True) + EPS) * on_ref[...]
            y_ref[0, rows(n), cols(h)] = (on * (zh * jax.nn.sigmoid(zh))).astype(BF16)


def _delta(q, k, v, z, gb, gt, on, *, tc):
    b, s, hd = q.shape
    n_chunk = tc // DN_CHUNK
    row = lambda width: pl.BlockSpec((1, tc, width), lambda i, j: (i, j, 0))
    return pl.pallas_call(
        functools.partial(_delta_kernel, n_chunk=n_chunk),
        grid=(b, s // tc),
        in_specs=[row(hd), row(hd), row(hd), row(hd), row(LANES),
                  pl.BlockSpec((1, n_chunk, DN_HEADS, DN_CHUNK), lambda i, j: (i, j, 0, 0)),
                  _const_spec(on.shape)],
        out_specs=row(hd),
        out_shape=jax.ShapeDtypeStruct((b, s, hd), BF16),
        scratch_shapes=[pltpu.VMEM((DN_HEADS, DN_D, DN_D), F32)],
        compiler_params=_params(("parallel", "arbitrary")),
        name="delta",
    )(q, k, v, z, gb, gt, on)


def _rope_tables(positions, dim):
    inv_freq = 1.0 / (ROPE_THETA ** (jnp.arange(0, dim, 2, dtype=F32) / dim))
    ang = positions.astype(F32)[..., None] * inv_freq
    return jnp.cos(ang), jnp.sin(ang)


def _row(v):
    return v.reshape(1, -1).astype(F32)


def kernel(x, p, positions, norm_mix_g, norm_ffn_g, ev_w_in, ev_conv_w, ev_q_norm_g, ev_k_norm_g, ev_ik_ln_g, ev_ik_ln_b, ev_w_out, od_w_in, od_conv_w, od_a_log, od_dt_bias, od_o_norm_g, od_w_out, ffn_w_gate, ffn_w_up, ffn_w_down, ple_w_proj, ple_post_norm_g, ple_norm_g, ple_w_gate):
    b, s, d = x.shape
    depth = p.shape[0]
    tm = min(512, s)

    cos_a, sin_a = _rope_tables(positions, HEAD)
    cos_i, sin_i = _rope_tables(positions, HEAD // 2)
    ca = jnp.concatenate([cos_a] * 4, -1)
    sa = jnp.concatenate([-sin_a, sin_a] * 2, -1)
    ones = jnp.ones(cos_i.shape[:-1] + (HEAD // 2,), F32)
    ci = jnp.concatenate([cos_i, cos_i, ones] * 2, -1)
    si = jnp.concatenate([-sin_i, sin_i, 0.0 * ones] * 2, -1)

    h = x
    for i in range(depth):
        j = i // 2
        if i % 2 == 0:
            w = ev_w_in[j]
            n_main = 3 * CONV_W + 3 * ATT_W
            w_pad = jnp.zeros((d, LANES - HEAD - N_HEADS), w.dtype)
            wx = jnp.concatenate([w[:, n_main:], w_pad], axis=1).astype(BF16)
            pad64 = lambda v: jnp.concatenate([v, jnp.zeros((HEAD,), v.dtype)]).reshape(1, LANES).astype(F32)
            ya, q, k, v, qi, kx, wi = _even_in(
                h, _row(norm_mix_g[i]), w[:, :n_main].astype(BF16), wx, ev_conv_w[j].astype(F32),
                _row(jnp.tile(ev_q_norm_g[j], N_HEADS)), _row(jnp.tile(ev_k_norm_g[j], N_HEADS)),
                pad64(ev_ik_ln_g[j]), pad64(ev_ik_ln_b[j]), ca, sa, ci, si, tm=tm)
            shift = (HEAD ** 0.5 * jnp.max(jnp.abs(ev_q_norm_g[j])) * jnp.max(jnp.abs(ev_k_norm_g[j]))).astype(F32)
            yb = _dsa(q, k, v, qi, kx, wi, shift.reshape(1, 1), tq=min(DSA_TQ, s))
            wo = ev_w_out[j].astype(BF16)
            ys = [ya.reshape(b * s, CONV_W), yb.reshape(b * s, ATT_W)]
        else:
            w = od_w_in[j]
            n_qkv = 3 * DN_HEADS * DN_D
            n_z = DN_HEADS * DN_D
            w_ab = w[:, n_qkv + n_z:]
            wab = jnp.concatenate([w_ab, jnp.zeros((d, LANES - 2 * DN_HEADS), w.dtype)], axis=1).astype(BF16)
            wabt = w_ab[:, :DN_HEADS].T.astype(BF16)
            a_log = od_a_log[j].astype(F32)
            dt_b = od_dt_bias[j].astype(F32)
            pad_row = lambda v: jnp.concatenate([v, jnp.zeros((LANES - DN_HEADS,), F32)]).reshape(1, LANES)
            q, k, v, z, gb, gt = _odd_in(
                h, _row(norm_mix_g[i]), w[:, :n_qkv].astype(BF16), w[:, n_qkv:n_qkv + n_z].astype(BF16),
                wab, wabt, od_conv_w[j].astype(F32), a_log.reshape(DN_HEADS, 1), dt_b.reshape(DN_HEADS, 1),
                pad_row(a_log), pad_row(dt_b), tm=tm)
            gt = gt.reshape(b, DN_HEADS, s // DN_CHUNK, DN_CHUNK).transpose(0, 2, 1, 3)
            y = _delta(q, k, v, z, gb, gt, _row(od_o_norm_g[j]), tc=min(DELTA_TILE, s))
            ys = [y.reshape(b * s, DN_HEADS * DN_D)]
            wo = od_w_out[j].astype(BF16)
        h = _post(h.reshape(b * s, d), ys, wo, p[i].reshape(b * s, -1), _row(norm_ffn_g[i]),
                  ffn_w_gate[i].astype(BF16), ffn_w_up[i].astype(BF16), ffn_w_down[i].astype(BF16),
                  ple_w_proj[i].astype(BF16), _row(ple_post_norm_g[i]), _row(ple_norm_g[i]),
                  ple_w_gate[i].astype(BF16), tm=tm).reshape(b, s, d)
    return h
```
